```python
import math
import jax, jax.numpy as jnp
from jax import lax
import numpy as np

D_MODEL = 1024
BATCH = 8
SEQ = 8192
DEPTH = 1

MEM_LEN = 256
HEAD_DIM = 64
N_ATTN_HEADS = 8
ROT_DIM = HEAD_DIM // 4
ROPE_THETA = 500000.0
DILATED_PATTERNS = ((128, 1), (512, 4), (2048, 16))
N_RET_HEADS = 4
RET_QK_DIM = 64
RET_V_DIM = 128
RET_CHUNK = 128
RET_THETA = 10000.0
N_MEM_HEADS = 4
MEM_HEAD_DIM = D_MODEL // N_MEM_HEADS
D_FF = 2816
EPS = 1e-6
NEG_INF = -1e30

ATTN_WIDTH = N_ATTN_HEADS * HEAD_DIM
RET_QK_WIDTH = N_RET_HEADS * RET_QK_DIM
RET_WIDTH = N_RET_HEADS * RET_V_DIM
MIX_WIDTH = ATTN_WIDTH + RET_WIDTH
IN_SPLITS = (ATTN_WIDTH, ATTN_WIDTH, ATTN_WIDTH, RET_QK_WIDTH, RET_QK_WIDTH, RET_WIDTH, RET_WIDTH)
IN_COLS = sum(IN_SPLITS)

kernel_name = "hymba_dilated_retention_macaron"


def rmsnorm(x, w):
    x32 = x.astype(jnp.float32)
    y = x32 * lax.rsqrt(jnp.mean(x32 * x32, axis=-1, keepdims=True) + EPS)
    return (y * w.astype(jnp.float32)).astype(x.dtype)


def rotary(x, pos, rot_dim, theta):
    half = rot_dim // 2
    inv = jnp.exp(-math.log(theta) * jnp.arange(half, dtype=jnp.float32) / half)
    ang = pos.astype(jnp.float32)[:, None] * inv[None, :]
    cos = jnp.cos(ang)[None, :, None, :]
    sin = jnp.sin(ang)[None, :, None, :]
    x1 = x[..., :half].astype(jnp.float32)
    x2 = x[..., half:rot_dim].astype(jnp.float32)
    rot = jnp.concatenate([x1 * cos - x2 * sin, x1 * sin + x2 * cos], axis=-1).astype(x.dtype)
    return jnp.concatenate([rot, x[..., rot_dim:]], axis=-1)


def dilated_window_attention(q, k, v, window, dilation):
    B, S, H, Dh = q.shape
    blk = window // dilation
    span = blk * dilation
    s_pad = -(-S // span) * span
    nb = s_pad // span

    def blocks(a):
        a = jnp.pad(a, ((0, 0), (0, s_pad - S), (0, 0), (0, 0)))
        return a.reshape(B, nb, blk, dilation, H, Dh)

    def with_prev(a):
        prev = jnp.pad(a, ((0, 0), (1, 0), (0, 0), (0, 0), (0, 0), (0, 0)))[:, :-1]
        return jnp.concatenate([prev, a], axis=2)

    qb = blocks(q)
    kk = with_prev(blocks(k))
    vv = with_prev(blocks(v))
    s = jnp.einsum('bnqrhd,bnkrhd->bnrhqk', qb, kk).astype(jnp.float32) * (Dh ** -0.5)
    qi = jnp.arange(blk)[:, None]
    kj = jnp.arange(2 * blk)[None, :]
    dist = blk + qi - kj
    band = (dist >= 0) & (dist <= blk)
    not_before_start = (jnp.arange(nb)[:, None, None] > 0) | (kj >= blk)[None]
    mask = band[None] & not_before_start
    s = jnp.where(mask[None, :, None, None], s, NEG_INF)
    m = jnp.max(s, axis=-1, keepdims=True)
    p = jnp.exp(s - m)
    denom = jnp.sum(p, axis=-1, keepdims=True)
    lse = (m + jnp.log(denom))[..., 0]
    o = jnp.einsum('bnrhqk,bnkrhd->bnqrhd', (p / denom).astype(v.dtype), vv)
    o = o.reshape(B, s_pad, H, Dh)[:, :S]
    lse = lse.transpose(0, 1, 4, 2, 3).reshape(B, s_pad, H)[:, :S]
    return o, lse


def mixture_of_dilations(q, k, v):
    outs, lses = [], []
    for window, dilation in DILATED_PATTERNS:
        o, lse = dilated_window_attention(q, k, v, window, dilation)
        outs.append(o)
        lses.append(lse)
    wts = jax.nn.softmax(jnp.stack(lses, axis=0), axis=0)
    o = jnp.einsum('gbsh,gbshd->bshd', wts, jnp.stack(outs, axis=0).astype(jnp.float32))
    return o.astype(q.dtype)


def retention_decays():
    return jnp.log1p(-(2.0 ** (-5.0 - jnp.arange(N_RET_HEADS, dtype=jnp.float32))))


def chunkwise_retention(q, k, v):
    B, S, H, dk = q.shape
    dv = v.shape[-1]
    C = RET_CHUNK
    N = S // C
    log_g = retention_decays()
    k = k * (dk ** -0.5)
    qc = q.reshape(B, N, C, H, dk)
    kc = k.reshape(B, N, C, H, dk)
    vc = v.reshape(B, N, C, H, dv)
    idx = jnp.arange(C, dtype=jnp.float32)
    diff = idx[:, None] - idx[None, :]
    decay_mask = jnp.where(diff[None] >= 0, jnp.exp(log_g[:, None, None] * jnp.maximum(diff, 0.0)[None]), 0.0)
    inner = jnp.einsum('bnqhd,bnkhd->bnhqk', qc, kc) * decay_mask[None, None]
    o_inner = jnp.einsum('bnhqk,bnkhe->bnqhe', inner, vc)
    k_decay = jnp.exp(log_g[:, None] * (C - 1 - idx)[None])
    kv = jnp.einsum('bnkhd,hk,bnkhe->bnhde', kc, k_decay, vc)
    chunk_decay = jnp.exp(log_g * C)[:, None, None]

    def step(R, kv_i):
        return chunk_decay * R + kv_i, R

    R0 = jnp.zeros((B, H, dk, dv), kv.dtype)
    _, R_prev = lax.scan(step, R0, jnp.moveaxis(kv, 1, 0))
    R_prev = jnp.moveaxis(R_prev, 0, 1)
    q_decay = jnp.exp(log_g[:, None] * (idx + 1.0)[None])
    o_cross = jnp.einsum('bnqhd,hq,bnhde->bnqhe', qc, q_decay, R_prev)
    return (o_inner + o_cross).reshape(B, S, H, dv)


def hybrid_mixer(h, w_in, w_out):
    B, S, _ = h.shape
    pos = jnp.arange(S)
    proj = h @ w_in
    aq, ak, av, rq, rk, rv, rg = jnp.split(proj, list(np.cumsum(IN_SPLITS)[:-1]), axis=-1)
    aq = rotary(aq.reshape(B, S, N_ATTN_HEADS, HEAD_DIM), pos, ROT_DIM, ROPE_THETA)
    ak = rotary(ak.reshape(B, S, N_ATTN_HEADS, HEAD_DIM), pos, ROT_DIM, ROPE_THETA)
    av = av.reshape(B, S, N_ATTN_HEADS, HEAD_DIM)
    o_attn = mixture_of_dilations(aq, ak, av).reshape(B, S, ATTN_WIDTH)
    rq = rotary(rq.reshape(B, S, N_RET_HEADS, RET_QK_DIM), pos, RET_QK_DIM, RET_THETA)
    rk = rotary(rk.reshape(B, S, N_RET_HEADS, RET_QK_DIM), pos, RET_QK_DIM, RET_THETA)
    rv = rv.reshape(B, S, N_RET_HEADS, RET_V_DIM)
    r = chunkwise_retention(rq, rk, rv).astype(jnp.float32)
    r = r * lax.rsqrt(jnp.mean(r * r, axis=-1, keepdims=True) + EPS)
    o_ret = (jax.nn.silu(rg.astype(jnp.float32)) * r.reshape(B, S, RET_WIDTH)).astype(h.dtype)
    return jnp.concatenate([o_attn, o_ret], axis=-1) @ w_out


def memory_cross_attention(h, mem_n, w_cq, w_ckv, w_co):
    B, S, _ = h.shape
    M = mem_n.shape[1]
    q = (h @ w_cq).reshape(B, S, N_MEM_HEADS, MEM_HEAD_DIM)
    k, v = jnp.split(mem_n @ w_ckv, 2, axis=-1)
    k = k.reshape(B, M, N_MEM_HEADS, MEM_HEAD_DIM)
    v = v.reshape(B, M, N_MEM_HEADS, MEM_HEAD_DIM)
    s = jnp.einsum('bshd,bmhd->bhsm', q, k).astype(jnp.float32) * (MEM_HEAD_DIM ** -0.5)
    p = jax.nn.softmax(s, axis=-1).astype(v.dtype)
    o = jnp.einsum('bhsm,bmhd->bshd', p, v).reshape(B, S, D_MODEL)
    return o @ w_co


def swiglu(h, w_in, w_out):
    g, u = jnp.split(h @ w_in, 2, axis=-1)
    return (jax.nn.silu(g) * u) @ w_out


def setup_inputs(seed: int = 0) -> dict:
    key = jax.random.key(seed)
    ks = jax.random.split(key, 20)

    def w(k, shape, fan_in):
        return jax.random.normal(k, shape, jnp.float32) * (fan_in ** -0.5)

    def gain(k, shape):
        return 1.0 + 0.01 * jax.random.normal(k, shape, jnp.float32)

    L = DEPTH
    return {
        "x": jax.random.normal(ks[0], (BATCH, SEQ, D_MODEL), jnp.float32),
        "mem": jax.random.normal(ks[1], (BATCH, MEM_LEN, D_MODEL), jnp.float32),
        "norm_ffn1": gain(ks[2], (L, D_MODEL)),
        "w_ffn1_in": w(ks[3], (L, D_MODEL, 2 * D_FF), D_MODEL),
        "w_ffn1_out": w(ks[4], (L, D_FF, D_MODEL), D_FF),
        "norm_mix": gain(ks[5], (L, D_MODEL)),
        "w_in": w(ks[6], (L, D_MODEL, IN_COLS), D_MODEL),
        "w_out": w(ks[7], (L, MIX_WIDTH, D_MODEL), MIX_WIDTH),
        "norm_cross": gain(ks[8], (L, D_MODEL)),
        "norm_mem": gain(ks[9], (L, D_MODEL)),
        "w_cq": w(ks[10], (L, D_MODEL, D_MODEL), D_MODEL),
        "w_ckv": w(ks[11], (L, D_MODEL, 2 * D_MODEL), D_MODEL),
        "w_co": w(ks[12], (L, D_MODEL, D_MODEL), D_MODEL),
        "norm_ffn2": gain(ks[13], (L, D_MODEL)),
        "w_ffn2_in": w(ks[14], (L, D_MODEL, 2 * D_FF), D_MODEL),
        "w_ffn2_out": w(ks[15], (L, D_FF, D_MODEL), D_FF),
        "norm_final": gain(ks[16], (D_MODEL,)),
    }


def reference(x, mem, norm_ffn1, w_ffn1_in, w_ffn1_out, norm_mix, w_in, w_out,
              norm_cross, norm_mem, w_cq, w_ckv, w_co, norm_ffn2, w_ffn2_in, w_ffn2_out,
              norm_final):
    for l in range(DEPTH):
        x = x + 0.5 * swiglu(rmsnorm(x, norm_ffn1[l]), w_ffn1_in[l], w_ffn1_out[l])
        x = x + hybrid_mixer(rmsnorm(x, norm_mix[l]), w_in[l], w_out[l])
        x = x + memory_cross_attention(rmsnorm(x, norm_cross[l]), rmsnorm(mem, norm_mem[l]),
                                       w_cq[l], w_ckv[l], w_co[l])
        x = x + 0.5 * swiglu(rmsnorm(x, norm_ffn2[l]), w_ffn2_in[l], w_ffn2_out[l])
    return rmsnorm(x, norm_final)
```

```python
import functools
import math

import jax
import jax.numpy as jnp
from jax import lax
from jax.experimental import pallas as pl
from jax.experimental.pallas import tpu as pltpu

F32 = jnp.float32
BF16 = jnp.bfloat16

HEAD_DIM = 64
N_ATTN_HEADS = 8
ROT_DIM = HEAD_DIM // 4
ROPE_THETA = 500000.0
DILATIONS = (1, 4, 16)
ATTN_BLK = 128
N_RET_HEADS = 4
RET_QK_DIM = 64
RET_V_DIM = 128
RET_CHUNK = 128
RET_THETA = 10000.0
N_MEM_HEADS = 4
EPS = 1e-6
NEG_INF = -1e30

ATTN_WIDTH = N_ATTN_HEADS * HEAD_DIM
RET_QK_WIDTH = N_RET_HEADS * RET_QK_DIM
RET_WIDTH = N_RET_HEADS * RET_V_DIM

V7X_LANES = 128
V7X_VMEM_BYTES = 64 * 1024 * 1024
VMEM_LIMIT = V7X_VMEM_BYTES - 8 * 1024 * 1024

TOKEN_TILE = 512
ATTN_POSITIONS = 1024
RET_TILE = 512


def _params(*sem):
    return pltpu.CompilerParams(dimension_semantics=sem, vmem_limit_bytes=VMEM_LIMIT)


def _const_spec(shape):
    nd = len(shape)
    return pl.BlockSpec(shape, lambda *_: (0,) * nd, pipeline_mode=pl.Buffered(1))


def _rmsnorm(x, w):
    ms = jnp.mean(x * x, axis=-1, keepdims=True)
    return x * lax.rsqrt(ms + EPS) * w


def _silu(x):
    return x * jax.nn.sigmoid(x)


def _ffn_kernel(x_ref, nw_ref, win_ref, wout_ref, *rest, d_ff, chunk, final_norm):
    if final_norm:
        fw_ref, o_ref, a_ref = rest
    else:
        o_ref, a_ref = rest
    x = x_ref[...]
    h = _rmsnorm(x, nw_ref[...]).astype(BF16)
    for c in range(d_ff // chunk):
        g = jnp.dot(h, win_ref[:, c * chunk:(c + 1) * chunk], preferred_element_type=F32)
        u = jnp.dot(h, win_ref[:, d_ff + c * chunk:d_ff + (c + 1) * chunk], preferred_element_type=F32)
        a_ref[:, c * chunk:(c + 1) * chunk] = (_silu(g) * u).astype(BF16)
    y = x + 0.5 * jnp.dot(a_ref[...], wout_ref[...], preferred_element_type=F32)
    if final_norm:
        y = _rmsnorm(y, fw_ref[...])
    o_ref[...] = y


def _ffn(x, norm_w, w_in, w_out, final_w=None):
    t, d = x.shape
    d_ff = w_out.shape[0]
    tm = TOKEN_TILE
    chunk = 256
    assert t % tm == 0 and d_ff % chunk == 0
    final_norm = final_w is not None
    in_specs = [
        pl.BlockSpec((tm, d), lambda i: (i, 0)),
        _const_spec((1, d)),
        _const_spec((d, 2 * d_ff)),
        _const_spec((d_ff, d)),
    ]
    args = [x, norm_w.reshape(1, d), w_in, w_out]
    if final_norm:
        in_specs.append(_const_spec((1, d)))
        args.append(final_w.reshape(1, d))
    return pl.pallas_call(
        functools.partial(_ffn_kernel, d_ff=d_ff, chunk=chunk, final_norm=final_norm),
        grid=(t // tm,),
        in_specs=in_specs,
        out_specs=pl.BlockSpec((tm, d), lambda i: (i, 0)),
        out_shape=jax.ShapeDtypeStruct((t, d), F32),
        scratch_shapes=[pltpu.VMEM((tm, d_ff), BF16)],
        compiler_params=_params("parallel"),
        name="ffn_final" if final_norm else "ffn",
    )(*args)


def _rotate(y, tab_ref, shift):
    n = y.shape[-1]
    return (y * tab_ref[0]
            + pltpu.roll(y, shift, 1) * tab_ref[1]
            + pltpu.roll(y, n - shift, 1) * tab_ref[2])


def _inproj_kernel(x_ref, nw_ref, w_ref, ta_ref, tr_ref,
                   aq_ref, ak_ref, av_ref, rq_ref, rk_ref, rv_ref, rg_ref):
    h = _rmsnorm(x_ref[...], nw_ref[...]).astype(BF16)

    def proj(lo, width):
        return jnp.dot(h, w_ref[:, lo:lo + width], preferred_element_type=F32)

    L = V7X_LANES
    a_half = ROT_DIM // 2
    r_half = RET_QK_DIM // 2
    y = proj(0, ATTN_WIDTH)
    for c in range(ATTN_WIDTH // L):
        aq_ref[:, c * L:(c + 1) * L] = (_rotate(y[:, c * L:(c + 1) * L], ta_ref, a_half)
                                        * (HEAD_DIM ** -0.5)).astype(BF16)
    y = proj(ATTN_WIDTH, ATTN_WIDTH)
    for c in range(ATTN_WIDTH // L):
        ak_ref[:, c * L:(c + 1) * L] = _rotate(y[:, c * L:(c + 1) * L], ta_ref, a_half).astype(BF16)
    av_ref[...] = proj(2 * ATTN_WIDTH, ATTN_WIDTH).astype(BF16)
    base = 3 * ATTN_WIDTH
    y = proj(base, 2 * RET_QK_WIDTH)
    for c in range(RET_QK_WIDTH // L):
        rq_ref[:, c * L:(c + 1) * L] = _rotate(y[:, c * L:(c + 1) * L], tr_ref, r_half).astype(BF16)
    for c in range(RET_QK_WIDTH // L):
        yc = y[:, RET_QK_WIDTH + c * L:RET_QK_WIDTH + (c + 1) * L]
        rk_ref[:, c * L:(c + 1) * L] = (_rotate(yc, tr_ref, r_half) * (RET_QK_DIM ** -0.5)).astype(BF16)
    base += 2 * RET_QK_WIDTH
    rv_ref[...] = proj(base, RET_WIDTH).astype(BF16)
    rg_ref[...] = proj(base + RET_WIDTH, RET_WIDTH)


def _rotary_tables(seq, rot_dim, head_dim, theta):
    half = rot_dim // 2
    inv = jnp.exp(-math.log(theta) * jnp.arange(half, dtype=F32) / half)
    ang = jnp.arange(seq).astype(F32)[:, None] * inv[None, :]
    cos, sin = jnp.cos(ang), jnp.sin(ang)
    pad = head_dim - rot_dim
    ones = jnp.ones((seq, pad), F32)
    zeros = jnp.zeros((seq, pad), F32)
    zh = jnp.zeros((seq, half), F32)
    c = jnp.concatenate([cos, cos, ones], axis=-1)
    s_plus = jnp.concatenate([zh, sin, zeros], axis=-1)
    s_minus = jnp.concatenate([-sin, zh, zeros], axis=-1)
    reps = V7X_LANES // head_dim
    return jnp.stack([jnp.tile(t, (1, reps)) for t in (c, s_plus, s_minus)], axis=0)


def _inproj(x, norm_w, w_in, seq):
    t, d = x.shape
    tm = TOKEN_TILE
    assert t % tm == 0 and seq % tm == 0
    n_s = seq // tm
    tab_a = _rotary_tables(seq, ROT_DIM, HEAD_DIM, ROPE_THETA)
    tab_r = _rotary_tables(seq, RET_QK_DIM, RET_QK_DIM, RET_THETA)
    widths = (ATTN_WIDTH, ATTN_WIDTH, ATTN_WIDTH, RET_QK_WIDTH, RET_QK_WIDTH, RET_WIDTH, RET_WIDTH)
    dtypes = (BF16,) * 6 + (F32,)
    tab_spec = pl.BlockSpec((3, tm, V7X_LANES), lambda i: (0, i % n_s, 0))
    return pl.pallas_call(
        _inproj_kernel,
        grid=(t // tm,),
        in_specs=[
            pl.BlockSpec((tm, d), lambda i: (i, 0)),
            _const_spec((1, d)),
            _const_spec(w_in.shape),
            tab_spec,
            tab_spec,
        ],
        out_specs=[pl.BlockSpec((tm, w), lambda i: (i, 0)) for w in widths],
        out_shape=[jax.ShapeDtypeStruct((t, w), dt) for w, dt in zip(widths, dtypes)],
        compiler_params=_params("parallel"),
        name="inproj",
    )(x, norm_w.reshape(1, d), w_in, tab_a, tab_r)


def _attn_kernel(*refs, n_qb, n_cc, has_in, write_lse):
    q_ref, k_ref, v_ref, kp_ref, vp_ref = refs[:5]
    refs = refs[5:]
    if has_in:
        oin_ref, lin_ref = refs[:2]
        refs = refs[2:]
    o_ref = refs[0]
    lout_ref = refs[1] if write_lse else None

    blk = ATTN_BLK
    L = V7X_LANES
    qi = lax.broadcasted_iota(jnp.int32, (blk, 2 * blk), 0)
    kj = lax.broadcasted_iota(jnp.int32, (blk, 2 * blk), 1)
    band = ((kj >= blk) & (kj - blk <= qi)) | ((kj < blk) & (kj >= qi))
    bias = jnp.where(band, 0.0, NEG_INF).astype(F32)
    no_prev = (pl.program_id(1) == 0).astype(F32)
    bias0 = bias + jnp.where(kj < blk, NEG_INF, 0.0).astype(F32) * no_prev
    low = lax.broadcasted_iota(jnp.int32, (blk, L), 1) < HEAD_DIM

    def chunk(c, carry):
        col = pl.ds(pl.multiple_of(c * L, L), L)
        for qb in range(n_qb):
            rows = slice(qb * blk, (qb + 1) * blk)
            q2 = q_ref[0, rows, col]
            if qb == 0:
                k2 = jnp.concatenate([kp_ref[0, :, col], k_ref[0, 0:blk, col]], axis=0)
                v2 = jnp.concatenate([vp_ref[0, :, col], v_ref[0, 0:blk, col]], axis=0)
                b = bias0
            else:
                k2 = k_ref[0, (qb - 1) * blk:(qb + 1) * blk, col]
                v2 = v_ref[0, (qb - 1) * blk:(qb + 1) * blk, col]
                b = bias
            outs, lses = [], []
            for hh in range(2):
                msk = low if hh == 0 else jnp.logical_not(low)
                qh = jnp.where(msk, q2, jnp.zeros_like(q2))
                s = lax.dot_general(qh, k2, (((1,), (1,)), ((), ())), preferred_element_type=F32) + b
                m = jnp.max(s, axis=-1, keepdims=True)
                p = jnp.exp(s - m)
                l = jnp.sum(p, axis=-1, keepdims=True)
                pv = jnp.dot(p.astype(BF16), v2, preferred_element_type=F32)
                outs.append(pv * (1.0 / l))
                lses.append(m + jnp.log(l))
            o_new = jnp.where(low, outs[0], outs[1])
            l_new = jnp.where(low, lses[0], lses[1])
            if has_in:
                o_old = oin_ref[0, rows, col].astype(F32)
                l_old = lin_ref[0, rows, col]
                e = jnp.exp(-jnp.abs(l_new - l_old))
                w_big = 1.0 / (1.0 + e)
                w_small = e * w_big
                new_big = l_new >= l_old
                w_new = jnp.where(new_big, w_big, w_small)
                w_old = jnp.where(new_big, w_small, w_big)
                o_new = o_new * w_new + o_old * w_old
                if write_lse:
                    l_new = jnp.maximum(l_new, l_old) + jnp.log(1.0 + e)
            o_ref[0, rows, col] = o_new.astype(o_ref.dtype)
            if write_lse:
                lout_ref[0, rows, col] = l_new
        return carry

    lax.fori_loop(0, n_cc, chunk, 0)


def _attn_pattern(q, k, v, dilation, prev, write_lse):
    b, s, w = q.shape
    blk = ATTN_BLK
    rows = s // dilation
    tq = max(blk, ATTN_POSITIONS // dilation)
    cw = w * (ATTN_POSITIONS // tq)
    cw = min(cw, dilation * w)
    assert rows % tq == 0 and (dilation * w) % cw == 0
    view = lambda a: a.reshape(b, rows, dilation * w)
    n_qb = tq // blk
    cur = pl.BlockSpec((1, tq, cw), lambda bi, i, j: (bi, i, j))
    prv = pl.BlockSpec((1, blk, cw), lambda bi, i, j: (bi, jnp.maximum(i * n_qb - 1, 0), j))
    has_in = prev is not None
    args = [view(q), view(k), view(v), view(k), view(v)]
    in_specs = [cur, cur, cur, prv, prv]
    if has_in:
        args += [view(prev[0]), view(prev[1])]
        in_specs += [cur, cur]
    out_shape = [jax.ShapeDtypeStruct((b, rows, dilation * w), BF16)]
    out_specs = [cur]
    if write_lse:
        out_shape.append(jax.ShapeDtypeStruct((b, rows, dilation * w), F32))
        out_specs.append(cur)
    outs = pl.pallas_call(
        functools.partial(_attn_kernel, n_qb=n_qb, n_cc=cw // V7X_LANES, has_in=has_in, write_lse=write_lse),
        grid=(b, rows // tq, (dilation * w) // cw),
        in_specs=in_specs,
        out_specs=out_specs,
        out_shape=out_shape,
        compiler_params=_params("parallel", "parallel", "parallel"),
        name=f"attn_d{dilation}",
    )(*args)
    return [o.reshape(b, s, w) for o in outs]


def _dilated_attention(q, k, v):
    state = None
    order = sorted(DILATIONS, reverse=True)
    for n, d in enumerate(order):
        last = n == len(order) - 1
        state = _attn_pattern(q, k, v, d, state, write_lse=not last)
    return state[0]


def _ret_kernel(q_ref, k_ref, v_ref, g_ref, dm_ref, qd_ref, kd_ref, cd_ref, o_ref, r_ref, *, n_chunks):
    C = RET_CHUNK
    L = V7X_LANES

    @pl.when(pl.program_id(1) == 0)
    def _():
        r_ref[...] = jnp.zeros_like(r_ref)

    low = lax.broadcasted_iota(jnp.int32, (C, L), 1) < RET_QK_DIM
    for c in range(n_chunks):
        rows = slice(c * C, (c + 1) * C)
        for h in range(N_RET_HEADS):
            p = h // 2
            msk = low if h % 2 == 0 else jnp.logical_not(low)
            q2 = q_ref[0, rows, p * L:(p + 1) * L]
            k2 = k_ref[0, rows, p * L:(p + 1) * L]
            qh = jnp.where(msk, q2, jnp.zeros_like(q2))
            kh = jnp.where(msk, k2, jnp.zeros_like(k2))
            vh = v_ref[0, rows, h * RET_V_DIM:(h + 1) * RET_V_DIM]
            inner = lax.dot_general(qh, k2, (((1,), (1,)), ((), ())), preferred_element_type=F32) * dm_ref[h]
            o = jnp.dot(inner.astype(BF16), vh, preferred_element_type=F32)
            r_old = r_ref[h]
            qdec = (qh.astype(F32) * qd_ref[h]).astype(BF16)
            o = o + jnp.dot(qdec, r_old.astype(BF16), preferred_element_type=F32)
            kdec = (kh.astype(F32) * kd_ref[h]).astype(BF16)
            kv = lax.dot_general(kdec, vh, (((0,), (0,)), ((), ())), preferred_element_type=F32)
            r_ref[h] = cd_ref[h] * r_old + kv
            rn = o * lax.rsqrt(jnp.mean(o * o, axis=-1, keepdims=True) + EPS)
            gate = g_ref[0, rows, h * RET_V_DIM:(h + 1) * RET_V_DIM]
            o_ref[0, rows, h * RET_V_DIM:(h + 1) * RET_V_DIM] = (_silu(gate) * rn).astype(o_ref.dtype)


def _retention_tables():
    C = RET_CHUNK
    log_g = jnp.log1p(-(2.0 ** (-5.0 - jnp.arange(N_RET_HEADS, dtype=F32))))
    idx = jnp.arange(C, dtype=F32)
    diff = idx[:, None] - idx[None, :]
    decay_mask = jnp.where(diff[None] >= 0, jnp.exp(log_g[:, None, None] * jnp.maximum(diff, 0.0)[None]), 0.0)
    k_decay = jnp.exp(log_g[:, None] * (C - 1 - idx)[None])
    q_decay = jnp.exp(log_g[:, None] * (idx + 1.0)[None])
    chunk_decay = jnp.exp(log_g * C)
    bc = lambda a: jnp.broadcast_to(a[:, :, None], (N_RET_HEADS, C, V7X_LANES))
    cd = jnp.broadcast_to(chunk_decay[:, None, None], (N_RET_HEADS, C, V7X_LANES))
    return decay_mask, bc(q_decay), bc(k_decay), cd


def _retention(rq, rk, rv, rg):
    b, s, _ = rq.shape
    tr = RET_TILE
    assert s % tr == 0 and tr % RET_CHUNK == 0 and RET_V_DIM == V7X_LANES and 2 * RET_QK_DIM == V7X_LANES
    dm, qd, kd, cd = _retention_tables()
    tile = lambda w: pl.BlockSpec((1, tr, w), lambda bi, i: (bi, i, 0))
    tab = _const_spec((N_RET_HEADS, RET_CHUNK, V7X_LANES))
    return pl.pallas_call(
        functools.partial(_ret_kernel, n_chunks=tr // RET_CHUNK),
        grid=(b, s // tr),
        in_specs=[tile(RET_QK_WIDTH), tile(RET_QK_WIDTH), tile(RET_WIDTH), tile(RET_WIDTH), tab, tab, tab, tab],
        out_specs=tile(RET_WIDTH),
        out_shape=jax.ShapeDtypeStruct((b, s, RET_WIDTH), BF16),
        scratch_shapes=[pltpu.VMEM((N_RET_HEADS, 2 * RET_QK_DIM, RET_V_DIM), F32)],
        compiler_params=_params("parallel", "arbitrary"),
        name="retention",
    )(rq, rk, rv, rg, dm, qd, kd, cd)


def _outproj_kernel(x_ref, oa_ref, or_ref, w_ref, o_ref):
    wa = oa_ref.shape[-1]
    y = jnp.dot(oa_ref[...], w_ref[0:wa, :], preferred_element_type=F32)
    y = y + jnp.dot(or_ref[...], w_ref[wa:, :], preferred_element_type=F32)
    o_ref[...] = x_ref[...] + y


def _outproj(x, o_attn, o_ret, w_out):
    t, d = x.shape
    tm = TOKEN_TILE
    row = lambda w: pl.BlockSpec((tm, w), lambda i: (i, 0))
    return pl.pallas_call(
        _outproj_kernel,
        grid=(t // tm,),
        in_specs=[row(d), row(o_attn.shape[-1]), row(o_ret.shape[-1]), _const_spec(w_out.shape)],
        out_specs=row(d),
        out_shape=jax.ShapeDtypeStruct((t, d), F32),
        compiler_params=_params("parallel"),
        name="outproj",
    )(x, o_attn, o_ret, w_out)


def _memkv_kernel(m_ref, nw_ref, w_ref, k_ref, v_ref):
    d = m_ref.shape[-1]
    h = _rmsnorm(m_ref[...], nw_ref[...]).astype(BF16)
    k_ref[...] = jnp.dot(h, w_ref[:, 0:d], preferred_element_type=F32).astype(BF16)
    v_ref[...] = jnp.dot(h, w_ref[:, d:], preferred_element_type=F32).astype(BF16)


def _memkv(mem, norm_w, w_ckv):
    t, d = mem.shape
    tm = min(TOKEN_TILE, t)
    assert t % tm == 0
    row = pl.BlockSpec((tm, d), lambda i: (i, 0))
    return pl.pallas_call(
        _memkv_kernel,
        grid=(t // tm,),
        in_specs=[row, _const_spec((1, d)), _const_spec(w_ckv.shape)],
        out_specs=[row, row],
        out_shape=[jax.ShapeDtypeStruct((t, d), BF16)] * 2,
        compiler_params=_params("parallel"),
        name="memkv",
    )(mem, norm_w.reshape(1, d), w_ckv)


def _cross_kernel(x_ref, nw_ref, wq_ref, k_ref, v_ref, wo_ref, o_ref, a_ref):
    x = x_ref[0]
    d = x.shape[-1]
    hd = d // N_MEM_HEADS
    h = _rmsnorm(x, nw_ref[...]).astype(BF16)
    q = (jnp.dot(h, wq_ref[...], preferred_element_type=F32) * (hd ** -0.5)).astype(BF16)
    for hh in range(N_MEM_HEADS):
        cols = slice(hh * hd, (hh + 1) * hd)
        s = lax.dot_general(q[:, cols], k_ref[0, :, cols], (((1,), (1,)), ((), ())), preferred_element_type=F32)
        m = jnp.max(s, axis=-1, keepdims=True)
        p = jnp.exp(s - m)
        l = jnp.sum(p, axis=-1, keepdims=True)
        pv = jnp.dot(p.astype(BF16), v_ref[0, :, cols], preferred_element_type=F32)
        a_ref[:, cols] = (pv * (1.0 / l)).astype(BF16)
    o_ref[0] = x + jnp.dot(a_ref[...], wo_ref[...], preferred_element_type=F32)


def _cross(x, norm_w, w_cq, mem_k, mem_v, w_co):
    b, s, d = x.shape
    m = mem_k.shape[1]
    tm = TOKEN_TILE
    assert s % tm == 0
    tile = pl.BlockSpec((1, tm, d), lambda bi, i: (bi, i, 0))
    kv = pl.BlockSpec((1, m, d), lambda bi, i: (bi, 0, 0))
    return pl.pallas_call(
        _cross_kernel,
        grid=(b, s // tm),
        in_specs=[tile, _const_spec((1, d)), _const_spec(w_cq.shape), kv, kv, _const_spec(w_co.shape)],
        out_specs=tile,
        out_shape=jax.ShapeDtypeStruct((b, s, d), F32),
        scratch_shapes=[pltpu.VMEM((tm, d), BF16)],
        compiler_params=_params("parallel", "parallel"),
        name="cross",
    )(x, norm_w.reshape(1, d), w_cq, mem_k, mem_v, w_co)


def kernel(x, mem, norm_ffn1, w_ffn1_in, w_ffn1_out, norm_mix, w_in, w_out, norm_cross, norm_mem,
           w_cq, w_ckv, w_co, norm_ffn2, w_ffn2_in, w_ffn2_out, norm_final):
    b, s, d = x.shape
    m = mem.shape[1]
    depth = w_in.shape[0]
    t = b * s
    bf = lambda a: a.astype(BF16)
    xt = x.reshape(t, d)
    for l in range(depth):
        last = l == depth - 1
        xt = _ffn(xt, norm_ffn1[l], bf(w_ffn1_in[l]), bf(w_ffn1_out[l]))
        aq, ak, av, rq, rk, rv, rg = _inproj(xt, norm_mix[l], bf(w_in[l]), s)
        sh = lambda a: a.reshape(b, s, a.shape[-1])
        o_attn = _dilated_attention(sh(aq), sh(ak), sh(av))
        o_ret = _retention(sh(rq), sh(rk), sh(rv), sh(rg))
        xt = _outproj(xt, o_attn.reshape(t, -1), o_ret.reshape(t, -1), bf(w_out[l]))
        mk, mv = _memkv(mem.reshape(b * m, d), norm_mem[l], bf(w_ckv[l]))
        xt = _cross(xt.reshape(b, s, d), norm_cross[l], bf(w_cq[l]), mk.reshape(b, m, d), mv.reshape(b, m, d),
                    bf(w_co[l])).reshape(t, d)
        xt = _ffn(xt, norm_ffn2[l], bf(w_ffn2_in[l]), bf(w_ffn2_out[l]), norm_final if last else None)
    if depth == 0:
        raise ValueError("depth must be positive")
    return xt.reshape(b, s, d)
```

```python
import functools
import math

import jax
import jax.numpy as jnp
from jax import lax
from jax.experimental import pallas as pl
from jax.experimental.pallas import tpu as pltpu

F32 = jnp.float32
BF16 = jnp.bfloat16

HEAD_DIM = 64
N_ATTN_HEADS = 8
ROT_DIM = HEAD_DIM // 4
ROPE_THETA = 500000.0
DILATIONS = (1, 4, 16)
ATTN_BLK = 128
N_RET_HEADS = 4
RET_QK_DIM = 64
RET_V_DIM = 128
RET_CHUNK = 128
RET_THETA = 10000.0
N_MEM_HEADS = 4
EPS = 1e-6
NEG_INF = -1e30
LOG2_E = 1.4426950408889634

ATTN_WIDTH = N_ATTN_HEADS * HEAD_DIM
RET_QK_WIDTH = N_RET_HEADS * RET_QK_DIM
RET_WIDTH = N_RET_HEADS * RET_V_DIM

V7X_LANES = 128
V7X_VMEM_BYTES = 64 * 1024 * 1024
VMEM_LIMIT = V7X_VMEM_BYTES - 8 * 1024 * 1024

TOKEN_TILE = 512
ATTN_TILE = ATTN_BLK * max(DILATIONS)
RET_TILE = 512


def _params(*sem):
    return pltpu.CompilerParams(dimension_semantics=sem, vmem_limit_bytes=VMEM_LIMIT)


def _const_spec(shape):
    nd = len(shape)
    return pl.BlockSpec(shape, lambda *_: (0,) * nd, pipeline_mode=pl.Buffered(1))


def _aligned(i, m):
    return i if isinstance(i, int) else pl.multiple_of(i, m)


def _rmsnorm(x, w):
    ms = jnp.mean(x * x, axis=-1, keepdims=True)
    return x * lax.rsqrt(ms + EPS) * w


def _silu(x):
    return x * jax.nn.sigmoid(x)


def _ffn_kernel(x_ref, nw_ref, win_ref, wout_ref, *rest, d_ff, chunk, final_norm):
    if final_norm:
        fw_ref, o_ref, a_ref = rest
    else:
        o_ref, a_ref = rest
    x = x_ref[...]
    h = _rmsnorm(x, nw_ref[...]).astype(BF16)
    for c in range(d_ff // chunk):
        g = jnp.dot(h, win_ref[:, c * chunk:(c + 1) * chunk], preferred_element_type=F32)
        u = jnp.dot(h, win_ref[:, d_ff + c * chunk:d_ff + (c + 1) * chunk], preferred_element_type=F32)
        a_ref[:, c * chunk:(c + 1) * chunk] = (_silu(g) * u).astype(BF16)
    y = x + 0.5 * jnp.dot(a_ref[...], wout_ref[...], preferred_element_type=F32)
    if final_norm:
        y = _rmsnorm(y, fw_ref[...])
    o_ref[...] = y


def _ffn(x, norm_w, w_in, w_out, final_w=None):
    t, d = x.shape
    d_ff = w_out.shape[0]
    tm = TOKEN_TILE
    chunk = 256
    assert t % tm == 0 and d_ff % chunk == 0
    final_norm = final_w is not None
    in_specs = [
        pl.BlockSpec((tm, d), lambda i: (i, 0)),
        _const_spec((1, d)),
        _const_spec((d, 2 * d_ff)),
        _const_spec((d_ff, d)),
    ]
    args = [x, norm_w.reshape(1, d), w_in, w_out]
    if final_norm:
        in_specs.append(_const_spec((1, d)))
        args.append(final_w.reshape(1, d))
    return pl.pallas_call(
        functools.partial(_ffn_kernel, d_ff=d_ff, chunk=chunk, final_norm=final_norm),
        grid=(t // tm,),
        in_specs=in_specs,
        out_specs=pl.BlockSpec((tm, d), lambda i: (i, 0)),
        out_shape=jax.ShapeDtypeStruct((t, d), F32),
        scratch_shapes=[pltpu.VMEM((tm, d_ff), BF16)],
        compiler_params=_params("parallel"),
        name="ffn_final" if final_norm else "ffn",
    )(*args)


def _rotate(y, tab_ref, shift):
    n = y.shape[-1]
    return (y * tab_ref[0]
            + pltpu.roll(y, shift, 1) * tab_ref[1]
            + pltpu.roll(y, n - shift, 1) * tab_ref[2])


def _inproj_kernel(x_ref, nw_ref, w_ref, ta_ref, tr_ref,
                   aq_ref, ak_ref, av_ref, rq_ref, rk_ref, rv_ref, rg_ref):
    h = _rmsnorm(x_ref[...], nw_ref[...]).astype(BF16)

    def proj(lo, width):
        return jnp.dot(h, w_ref[:, lo:lo + width], preferred_element_type=F32)

    L = V7X_LANES
    a_half = ROT_DIM // 2
    r_half = RET_QK_DIM // 2
    y = proj(0, ATTN_WIDTH)
    for c in range(ATTN_WIDTH // L):
        aq_ref[:, c * L:(c + 1) * L] = (_rotate(y[:, c * L:(c + 1) * L], ta_ref, a_half)
                                        * (HEAD_DIM ** -0.5)) * LOG2_E
    y = proj(ATTN_WIDTH, ATTN_WIDTH)
    for c in range(ATTN_WIDTH // L):
        ak_ref[:, c * L:(c + 1) * L] = _rotate(y[:, c * L:(c + 1) * L], ta_ref, a_half)
    av_ref[...] = proj(2 * ATTN_WIDTH, ATTN_WIDTH)
    base = 3 * ATTN_WIDTH
    y = proj(base, 2 * RET_QK_WIDTH)
    for c in range(RET_QK_WIDTH // L):
        rq_ref[:, c * L:(c + 1) * L] = _rotate(y[:, c * L:(c + 1) * L], tr_ref, r_half).astype(BF16)
    for c in range(RET_QK_WIDTH // L):
        yc = y[:, RET_QK_WIDTH + c * L:RET_QK_WIDTH + (c + 1) * L]
        rk_ref[:, c * L:(c + 1) * L] = (_rotate(yc, tr_ref, r_half) * (RET_QK_DIM ** -0.5)).astype(BF16)
    base += 2 * RET_QK_WIDTH
    rv_ref[...] = proj(base, RET_WIDTH).astype(BF16)
    rg_ref[...] = proj(base + RET_WIDTH, RET_WIDTH)


def _rotary_tables(seq, rot_dim, head_dim, theta):
    half = rot_dim // 2
    inv = jnp.exp(-math.log(theta) * jnp.arange(half, dtype=F32) / half)
    ang = jnp.arange(seq).astype(F32)[:, None] * inv[None, :]
    cos, sin = jnp.cos(ang), jnp.sin(ang)
    pad = head_dim - rot_dim
    ones = jnp.ones((seq, pad), F32)
    zeros = jnp.zeros((seq, pad), F32)
    zh = jnp.zeros((seq, half), F32)
    c = jnp.concatenate([cos, cos, ones], axis=-1)
    s_plus = jnp.concatenate([zh, sin, zeros], axis=-1)
    s_minus = jnp.concatenate([-sin, zh, zeros], axis=-1)
    reps = V7X_LANES // head_dim
    return jnp.stack([jnp.tile(t, (1, reps)) for t in (c, s_plus, s_minus)], axis=0)


def _inproj(x, norm_w, w_in, seq):
    t, d = x.shape
    tm = TOKEN_TILE
    assert t % tm == 0 and seq % tm == 0
    n_s = seq // tm
    tab_a = _rotary_tables(seq, ROT_DIM, HEAD_DIM, ROPE_THETA)
    tab_r = _rotary_tables(seq, RET_QK_DIM, RET_QK_DIM, RET_THETA)
    widths = (ATTN_WIDTH, ATTN_WIDTH, ATTN_WIDTH, RET_QK_WIDTH, RET_QK_WIDTH, RET_WIDTH, RET_WIDTH)
    dtypes = (F32, F32, F32, BF16, BF16, BF16, F32)
    tab_spec = pl.BlockSpec((3, tm, V7X_LANES), lambda i: (0, i % n_s, 0))
    return pl.pallas_call(
        _inproj_kernel,
        grid=(t // tm,),
        in_specs=[
            pl.BlockSpec((tm, d), lambda i: (i, 0)),
            _const_spec((1, d)),
            _const_spec(w_in.shape),
            tab_spec,
            tab_spec,
        ],
        out_specs=[pl.BlockSpec((tm, w), lambda i: (i, 0)) for w in widths],
        out_shape=[jax.ShapeDtypeStruct((t, w), dt) for w, dt in zip(widths, dtypes)],
        compiler_params=_params("parallel"),
        name="inproj",
    )(x, norm_w.reshape(1, d), w_in, tab_a, tab_r)


def _attn_kernel(q_ref, k_ref, v_ref, bias_ref, o_ref,
                 qs_ref, ks_ref, vs_ref, d4_ref, acc_ref, l_ref, m_ref):
    blk = ATTN_BLK
    L = V7X_LANES
    T = ATTN_TILE
    n_pat = len(DILATIONS)
    t = pl.program_id(2)
    cur = t % 2
    prev = 1 - cur
    first_tile = (t == 0).astype(jnp.int32)

    @pl.when(t == 0)
    def _():
        ks_ref[:, 1] = jnp.zeros(ks_ref.shape[:1] + ks_ref.shape[2:], BF16)
        vs_ref[:, 1, :, 0:L] = jnp.zeros((vs_ref.shape[0], T, L), BF16)
        vs_ref[:, :, :, L:2 * L] = jnp.ones(vs_ref.shape[:3] + (L,), BF16)

    def low_mask(n):
        return lax.broadcasted_iota(jnp.int32, (n, L), 1) < HEAD_DIM

    def put(kind, pat, row0, x):
        n = x.shape[0]
        rows = slice(row0, row0 + n)
        if kind == "q":
            xa = jnp.where(low_mask(n), x, 0.0)
            qs_ref[pat, 0, rows, :] = xa.astype(BF16)
            qs_ref[pat, 1, rows, :] = (x - xa).astype(BF16)
        elif kind == "k":
            ks_ref[pat, cur, rows, :] = x.astype(BF16)
        else:
            vs_ref[pat, cur, rows, 0:L] = x.astype(BF16)

    ch = 256
    n4 = T // 4
    n16 = T // 16
    for kind, ref in (("q", q_ref), ("k", k_ref), ("v", v_ref)):
        for c in range(T // ch):
            put(kind, 0, c * ch, ref[0, c * ch:(c + 1) * ch, :])
        for r in range(4):
            for c in range(n4 // ch):
                x = ref[0, pl.ds(r + 4 * ch * c, ch, stride=4), :]
                d4_ref[r * n4 + c * ch:r * n4 + (c + 1) * ch, :] = x
                put(kind, 1, r * n4 + c * ch, x)
        for r16 in range(16):
            x = d4_ref[pl.ds((r16 % 4) * n4 + r16 // 4, n16, stride=4), :]
            put(kind, 2, r16 * n16, x)

    low = low_mask(blk)

    def unit(pat, d, r, qb, qb_is_zero):
        n_r = T // d
        base = _aligned(r * n_r + qb * blk, blk)
        own = pl.ds(base, blk)
        qq = jnp.concatenate([qs_ref[pat, 0, own, :], qs_ref[pat, 1, own, :]], axis=0)
        if qb_is_zero:
            pslot = prev
            prow = pl.ds(_aligned(r * n_r + n_r - blk, blk), blk)
            bias = bias_ref[first_tile]
        else:
            pslot = cur
            prow = pl.ds(_aligned(base - blk, blk), blk)
            bias = bias_ref[0]
        k2 = jnp.concatenate([ks_ref[pat, pslot, prow, :], ks_ref[pat, cur, own, :]], axis=0)
        v3 = jnp.concatenate([vs_ref[pat, pslot, prow, :], vs_ref[pat, cur, own, :]], axis=0)
        s = lax.dot_general(qq, k2, (((1,), (1,)), ((), ())), preferred_element_type=F32) + bias
        m = jnp.max(s, axis=-1, keepdims=True)
        p = jnp.exp2(s - m).astype(BF16)
        res = jnp.dot(p, v3, preferred_element_type=F32)
        acc = jnp.where(low, res[0:blk, 0:L], res[blk:, 0:L])
        den = jnp.where(low, res[0:blk, L:], res[blk:, L:])
        m2 = jnp.where(low, m[0:blk], m[blk:])
        if d > 1:
            pos = pl.ds(qb * blk * d + r, blk, stride=d)
            st = pat - 1
            acc_ref[st, pos, :] = acc
            l_ref[st, pos, :] = den
            m_ref[st, pos, :] = m2
        else:
            pos = pl.ds(_aligned(qb * blk, blk), blk)
            m_tot = m2
            for st in range(n_pat - 1):
                m_tot = jnp.maximum(m_tot, m_ref[st, pos, :])
            w = jnp.exp2(m2 - m_tot)
            acc = w * acc
            den = w * den
            for st in range(n_pat - 1):
                w = jnp.exp2(m_ref[st, pos, :] - m_tot)
                acc = acc + w * acc_ref[st, pos, :]
                den = den + w * l_ref[st, pos, :]
            o_ref[0, pos, :] = (acc * (1.0 / den)).astype(o_ref.dtype)

    per_body = 4

    def body16(g, carry):
        for j in range(per_body):
            unit(2, 16, g * per_body + j, 0, True)
        return carry

    lax.fori_loop(0, 16 // per_body, body16, 0)

    def body4(r, carry):
        for qb in range(n4 // blk):
            unit(1, 4, r, qb, qb == 0)
        return carry

    lax.fori_loop(0, 4, body4, 0)

    unit(0, 1, 0, 0, True)
    group = 5

    def body1(g, carry):
        for j in range(group):
            unit(0, 1, 0, 1 + g * group + j, False)
        return carry

    lax.fori_loop(0, (T // blk - 1) // group, body1, 0)


def _attn_bias():
    blk = ATTN_BLK
    qi = jnp.arange(blk)[:, None]
    kj = jnp.arange(2 * blk)[None, :]
    dist = blk + qi - kj
    band = (dist >= 0) & (dist <= blk)
    bias = jnp.where(band, 0.0, NEG_INF).astype(F32)
    bias0 = jnp.where(band & (kj >= blk), 0.0, NEG_INF).astype(F32)
    return jnp.stack([jnp.tile(bias, (2, 1)), jnp.tile(bias0, (2, 1))], axis=0)


def _dilated_attention(q, k, v):
    b, s, w = q.shape
    T = ATTN_TILE
    L = V7X_LANES
    assert DILATIONS == (1, 4, 16) and 2 * HEAD_DIM == L and s % T == 0 and (T // ATTN_BLK - 1) % 5 == 0
    n_pat = len(DILATIONS)
    tile = pl.BlockSpec((1, T, L), lambda bi, p, t: (bi, t, p))
    return pl.pallas_call(
        _attn_kernel,
        grid=(b, w // L, s // T),
        in_specs=[tile, tile, tile, _const_spec((2, 2 * ATTN_BLK, 2 * ATTN_BLK))],
        out_specs=tile,
        out_shape=jax.ShapeDtypeStruct((b, s, w), BF16),
        scratch_shapes=[
            pltpu.VMEM((n_pat, 2, T, L), BF16),
            pltpu.VMEM((n_pat, 2, T, L), BF16),
            pltpu.VMEM((n_pat, 2, T, 2 * L), BF16),
            pltpu.VMEM((T, L), F32),
            pltpu.VMEM((n_pat - 1, T, L), F32),
            pltpu.VMEM((n_pat - 1, T, L), F32),
            pltpu.VMEM((n_pat - 1, T, L), F32),
        ],
        compiler_params=_params("parallel", "parallel", "arbitrary"),
        name="attn",
    )(q, k, v, _attn_bias())


def _ret_kernel(q_ref, k_ref, v_ref, g_ref, dm_ref, qd_ref, kd_ref, cd_ref, o_ref, r_ref, *, n_chunks):
    C = RET_CHUNK
    L = V7X_LANES

    @pl.when(pl.program_id(1) == 0)
    def _():
        r_ref[...] = jnp.zeros_like(r_ref)

    low = lax.broadcasted_iota(jnp.int32, (C, L), 1) < RET_QK_DIM
    for c in range(n_chunks):
        rows = slice(c * C, (c + 1) * C)
        for h in range(N_RET_HEADS):
            p = h // 2
            msk = low if h % 2 == 0 else jnp.logical_not(low)
            q2 = q_ref[0, rows, p * L:(p + 1) * L]
            k2 = k_ref[0, rows, p * L:(p + 1) * L]
            qh = jnp.where(msk, q2, jnp.zeros_like(q2))
            kh = jnp.where(msk, k2, jnp.zeros_like(k2))
            vh = v_ref[0, rows, h * RET_V_DIM:(h + 1) * RET_V_DIM]
            inner = lax.dot_general(qh, k2, (((1,), (1,)), ((), ())), preferred_element_type=F32) * dm_ref[h]
            o = jnp.dot(inner.astype(BF16), vh, preferred_element_type=F32)
            r_old = r_ref[h]
            qdec = (qh.astype(F32) * qd_ref[h]).astype(BF16)
            o = o + jnp.dot(qdec, r_old.astype(BF16), preferred_element_type=F32)
            kdec = (kh.astype(F32) * kd_ref[h]).astype(BF16)
            kv = lax.dot_general(kdec, vh, (((0,), (0,)), ((), ())), preferred_element_type=F32)
            r_ref[h] = cd_ref[h] * r_old + kv
            rn = o * lax.rsqrt(jnp.mean(o * o, axis=-1, keepdims=True) + EPS)
            gate = g_ref[0, rows, h * RET_V_DIM:(h + 1) * RET_V_DIM]
            o_ref[0, rows, h * RET_V_DIM:(h + 1) * RET_V_DIM] = (_silu(gate) * rn).astype(o_ref.dtype)


def _retention_tables():
    C = RET_CHUNK
    log_g = jnp.log1p(-(2.0 ** (-5.0 - jnp.arange(N_RET_HEADS, dtype=F32))))
    idx = jnp.arange(C, dtype=F32)
    diff = idx[:, None] - idx[None, :]
    decay_mask = jnp.where(diff[None] >= 0, jnp.exp(log_g[:, None, None] * jnp.maximum(diff, 0.0)[None]), 0.0)
    k_decay = jnp.exp(log_g[:, None] * (C - 1 - idx)[None])
    q_decay = jnp.exp(log_g[:, None] * (idx + 1.0)[None])
    chunk_decay = jnp.exp(log_g * C)
    bc = lambda a: jnp.broadcast_to(a[:, :, None], (N_RET_HEADS, C, V7X_LANES))
    cd = jnp.broadcast_to(chunk_decay[:, None, None], (N_RET_HEADS, C, V7X_LANES))
    return decay_mask, bc(q_decay), bc(k_decay), cd


def _retention(rq, rk, rv, rg):
    b, s, _ = rq.shape
    tr = RET_TILE
    assert s % tr == 0 and tr % RET_CHUNK == 0 and RET_V_DIM == V7X_LANES and 2 * RET_QK_DIM == V7X_LANES
    dm, qd, kd, cd = _retention_tables()
    tile = lambda w: pl.BlockSpec((1, tr, w), lambda bi, i: (bi, i, 0))
    tab = _const_spec((N_RET_HEADS, RET_CHUNK, V7X_LANES))
    return pl.pallas_call(
        functools.partial(_ret_kernel, n_chunks=tr // RET_CHUNK),
        grid=(b, s // tr),
        in_specs=[tile(RET_QK_WIDTH), tile(RET_QK_WIDTH), tile(RET_WIDTH), tile(RET_WIDTH), tab, tab, tab, tab],
        out_specs=tile(RET_WIDTH),
        out_shape=jax.ShapeDtypeStruct((b, s, RET_WIDTH), BF16),
        scratch_shapes=[pltpu.VMEM((N_RET_HEADS, 2 * RET_QK_DIM, RET_V_DIM), F32)],
        compiler_params=_params("parallel", "arbitrary"),
        name="retention",
    )(rq, rk, rv, rg, dm, qd, kd, cd)


def _outproj_kernel(x_ref, oa_ref, or_ref, w_ref, o_ref):
    wa = oa_ref.shape[-1]
    y = jnp.dot(oa_ref[...], w_ref[0:wa, :], preferred_element_type=F32)
    y = y + jnp.dot(or_ref[...], w_ref[wa:, :], preferred_element_type=F32)
    o_ref[...] = x_ref[...] + y


def _outproj(x, o_attn, o_ret, w_out):
    t, d = x.shape
    tm = TOKEN_TILE
    row = lambda w: pl.BlockSpec((tm, w), lambda i: (i, 0))
    return pl.pallas_call(
        _outproj_kernel,
        grid=(t // tm,),
        in_specs=[row(d), row(o_attn.shape[-1]), row(o_ret.shape[-1]), _const_spec(w_out.shape)],
        out_specs=row(d),
        out_shape=jax.ShapeDtypeStruct((t, d), F32),
        compiler_params=_params("parallel"),
        name="outproj",
    )(x, o_attn, o_ret, w_out)


def _memkv_kernel(m_ref, nw_ref, w_ref, k_ref, v_ref):
    d = m_ref.shape[-1]
    h = _rmsnorm(m_ref[...], nw_ref[...]).astype(BF16)
    k_ref[...] = jnp.dot(h, w_ref[:, 0:d], preferred_element_type=F32).astype(BF16)
    v_ref[...] = jnp.dot(h, w_ref[:, d:], preferred_element_type=F32).astype(BF16)


def _memkv(mem, norm_w, w_ckv):
    t, d = mem.shape
    tm = min(TOKEN_TILE, t)
    assert t % tm == 0
    row = pl.BlockSpec((tm, d), lambda i: (i, 0))
    return pl.pallas_call(
        _memkv_kernel,
        grid=(t // tm,),
        in_specs=[row, _const_spec((1, d)), _const_spec(w_ckv.shape)],
        out_specs=[row, row],
        out_shape=[jax.ShapeDtypeStruct((t, d), BF16)] * 2,
        compiler_params=_params("parallel"),
        name="memkv",
    )(mem, norm_w.reshape(1, d), w_ckv)


def _cross_kernel(x_ref, nw_ref, wq_ref, k_ref, v_ref, wo_ref, o_ref, a_ref):
    x = x_ref[0]
    d = x.shape[-1]
    hd = d // N_MEM_HEADS
    h = _rmsnorm(x, nw_ref[...]).astype(BF16)
    q = (jnp.dot(h, wq_ref[...], preferred_element_type=F32) * (hd ** -0.5)).astype(BF16)
    for hh in range(N_MEM_HEADS):
        cols = slice(hh * hd, (hh + 1) * hd)
        s = lax.dot_general(q[:, cols], k_ref[0, :, cols], (((1,), (1,)), ((), ())), preferred_element_type=F32)
        m = jnp.max(s, axis=-1, keepdims=True)
        p = jnp.exp(s - m)
        l = jnp.sum(p, axis=-1, keepdims=True)
        pv = jnp.dot(p.astype(BF16), v_ref[0, :, cols], preferred_element_type=F32)
        a_ref[:, cols] = (pv * (1.0 / l)).astype(BF16)
    o_ref[0] = x + jnp.dot(a_ref[...], wo_ref[...], preferred_element_type=F32)


def _cross(x, norm_w, w_cq, mem_k, mem_v, w_co):
    b, s, d = x.shape
    m = mem_k.shape[1]
    tm = TOKEN_TILE
    assert s % tm == 0
    tile = pl.BlockSpec((1, tm, d), lambda bi, i: (bi, i, 0))
    kv = pl.BlockSpec((1, m, d), lambda bi, i: (bi, 0, 0))
    return pl.pallas_call(
        _cross_kernel,
        grid=(b, s // tm),
        in_specs=[tile, _const_spec((1, d)), _const_spec(w_cq.shape), kv, kv, _const_spec(w_co.shape)],
        out_specs=tile,
        out_shape=jax.ShapeDtypeStruct((b, s, d), F32),
        scratch_shapes=[pltpu.VMEM((tm, d), BF16)],
        compiler_params=_params("parallel", "parallel"),
        name="cross",
    )(x, norm_w.reshape(1, d), w_cq, mem_k, mem_v, w_co)


def kernel(x, mem, norm_ffn1, w_ffn1_in, w_ffn1_out, norm_mix, w_in, w_out, norm_cross, norm_mem,
           w_cq, w_ckv, w_co, norm_ffn2, w_ffn2_in, w_ffn2_out, norm_final):
    b, s, d = x.shape
    m = mem.shape[1]
    depth = w_in.shape[0]
    assert depth >= 1
    t = b * s
    bf = lambda a: a.astype(BF16)
    xt = x.reshape(t, d)
    for l in range(depth):
        last = l == depth - 1
        xt = _ffn(xt, norm_ffn1[l], bf(w_ffn1_in[l]), bf(w_ffn1_out[l]))
        aq, ak, av, rq, rk, rv, rg = _inproj(xt, norm_mix[l], bf(w_in[l]), s)
        sh = lambda a: a.reshape(b, s, a.shape[-1])
        o_attn = _dilated_attention(sh(aq), sh(ak), sh(av))
        o_ret = _retention(sh(rq), sh(rk), sh(rv), sh(rg))
        xt = _outproj(xt, o_attn.reshape(t, -1), o_ret.reshape(t, -1), bf(w_out[l]))
        mk, mv = _memkv(mem.reshape(b * m, d), norm_mem[l], bf(w_ckv[l]))
        xt = _cross(xt.reshape(b, s, d), norm_cross[l], bf(w_cq[l]), mk.reshape(b, m, d), mv.reshape(b, m, d),
                    bf(w_co[l])).reshape(t, d)
        xt = _ffn(xt, norm_ffn2[l], bf(w_ffn2_in[l]), bf(w_ffn2_out[l]), norm_final if last else None)
    return xt.reshape(b, s, d)
```

```python
import functools
import math

import jax
import jax.numpy as jnp
from jax import lax
from jax.experimental import pallas as pl
from jax.experimental.pallas import tpu as pltpu

F32 = jnp.float32
BF16 = jnp.bfloat16

HEAD_DIM = 64
N_ATTN_HEADS = 8
ROT_DIM = HEAD_DIM // 4
ROPE_THETA = 500000.0
DILATIONS = (1, 4, 16)
ATTN_BLK = 128
N_RET_HEADS = 4
RET_QK_DIM = 64
RET_V_DIM = 128
RET_CHUNK = 128
RET_THETA = 10000.0
N_MEM_HEADS = 4
EPS = 1e-6
NEG_INF = -1e30
LOG2_E = 1.4426950408889634

ATTN_WIDTH = N_ATTN_HEADS * HEAD_DIM
RET_QK_WIDTH = N_RET_HEADS * RET_QK_DIM
RET_WIDTH = N_RET_HEADS * RET_V_DIM

V7X_LANES = 128
V7X_VMEM_BYTES = 64 * 1024 * 1024
VMEM_LIMIT = V7X_VMEM_BYTES - 8 * 1024 * 1024

TOKEN_TILE = 512
ATTN_TILE = ATTN_BLK * max(DILATIONS)
ATTN_UNROLL = 8
RET_TILE = 1024


def _params(*sem):
    return pltpu.CompilerParams(dimension_semantics=sem, vmem_limit_bytes=VMEM_LIMIT)


def _const_spec(shape):
    nd = len(shape)
    return pl.BlockSpec(shape, lambda *_: (0,) * nd, pipeline_mode=pl.Buffered(1))


def _aligned(i, m):
    return i if isinstance(i, int) else pl.multiple_of(i, m)


def _rmsnorm(x, w):
    ms = jnp.mean(x * x, axis=-1, keepdims=True)
    return x * lax.rsqrt(ms + EPS) * w


def _silu(x):
    return x * jax.nn.sigmoid(x)


def _ffn_kernel(x_ref, nw_ref, win_ref, wout_ref, *rest, d_ff, chunk, final_norm):
    if final_norm:
        fw_ref, o_ref, a_ref = rest
    else:
        o_ref, a_ref = rest
    x = x_ref[...]
    h = _rmsnorm(x, nw_ref[...]).astype(BF16)
    for c in range(d_ff // chunk):
        g = jnp.dot(h, win_ref[:, c * chunk:(c + 1) * chunk], preferred_element_type=F32)
        u = jnp.dot(h, win_ref[:, d_ff + c * chunk:d_ff + (c + 1) * chunk], preferred_element_type=F32)
        a_ref[:, c * chunk:(c + 1) * chunk] = (_silu(g) * u).astype(BF16)
    y = x + 0.5 * jnp.dot(a_ref[...], wout_ref[...], preferred_element_type=F32)
    if final_norm:
        y = _rmsnorm(y, fw_ref[...])
    o_ref[...] = y


def _ffn(x, norm_w, w_in, w_out, final_w=None):
    t, d = x.shape
    d_ff = w_out.shape[0]
    tm = TOKEN_TILE
    chunk = 256
    assert t % tm == 0 and d_ff % chunk == 0
    final_norm = final_w is not None
    in_specs = [
        pl.BlockSpec((tm, d), lambda i: (i, 0)),
        _const_spec((1, d)),
        _const_spec((d, 2 * d_ff)),
        _const_spec((d_ff, d)),
    ]
    args = [x, norm_w.reshape(1, d), w_in, w_out]
    if final_norm:
        in_specs.append(_const_spec((1, d)))
        args.append(final_w.reshape(1, d))
    return pl.pallas_call(
        functools.partial(_ffn_kernel, d_ff=d_ff, chunk=chunk, final_norm=final_norm),
        grid=(t // tm,),
        in_specs=in_specs,
        out_specs=pl.BlockSpec((tm, d), lambda i: (i, 0)),
        out_shape=jax.ShapeDtypeStruct((t, d), F32),
        scratch_shapes=[pltpu.VMEM((tm, d_ff), BF16)],
        compiler_params=_params("parallel"),
        name="ffn_final" if final_norm else "ffn",
    )(*args)


def _rotate(y, tab_ref, shift):
    n = y.shape[-1]
    return (y * tab_ref[0]
            + pltpu.roll(y, shift, 1) * tab_ref[1]
            + pltpu.roll(y, n - shift, 1) * tab_ref[2])


def _inproj_kernel(x_ref, nw_ref, w_ref, ta_ref, tr_ref,
                   aq_ref, ak_ref, av_ref, rq_ref, rk_ref, rv_ref, rg_ref):
    h = _rmsnorm(x_ref[...], nw_ref[...]).astype(BF16)

    def proj(lo, width):
        return jnp.dot(h, w_ref[:, lo:lo + width], preferred_element_type=F32)

    L = V7X_LANES
    a_half = ROT_DIM // 2
    r_half = RET_QK_DIM // 2
    y = proj(0, ATTN_WIDTH)
    for c in range(ATTN_WIDTH // L):
        aq_ref[:, c * L:(c + 1) * L] = (_rotate(y[:, c * L:(c + 1) * L], ta_ref, a_half)
                                        * (HEAD_DIM ** -0.5)) * LOG2_E
    y = proj(ATTN_WIDTH, ATTN_WIDTH)
    for c in range(ATTN_WIDTH // L):
        ak_ref[:, c * L:(c + 1) * L] = _rotate(y[:, c * L:(c + 1) * L], ta_ref, a_half)
    av_ref[...] = proj(2 * ATTN_WIDTH, ATTN_WIDTH)
    base = 3 * ATTN_WIDTH
    y = proj(base, 2 * RET_QK_WIDTH)
    for c in range(RET_QK_WIDTH // L):
        rq_ref[:, c * L:(c + 1) * L] = _rotate(y[:, c * L:(c + 1) * L], tr_ref, r_half).astype(BF16)
    for c in range(RET_QK_WIDTH // L):
        yc = y[:, RET_QK_WIDTH + c * L:RET_QK_WIDTH + (c + 1) * L]
        rk_ref[:, c * L:(c + 1) * L] = (_rotate(yc, tr_ref, r_half) * (RET_QK_DIM ** -0.5)).astype(BF16)
    base += 2 * RET_QK_WIDTH
    rv_ref[...] = proj(base, RET_WIDTH).astype(BF16)
    rg_ref[...] = proj(base + RET_WIDTH, RET_WIDTH)


def _rotary_tables(seq, rot_dim, head_dim, theta):
    half = rot_dim // 2
    inv = jnp.exp(-math.log(theta) * jnp.arange(half, dtype=F32) / half)
    ang = jnp.arange(seq).astype(F32)[:, None] * inv[None, :]
    cos, sin = jnp.cos(ang), jnp.sin(ang)
    pad = head_dim - rot_dim
    ones = jnp.ones((seq, pad), F32)
    zeros = jnp.zeros((seq, pad), F32)
    zh = jnp.zeros((seq, half), F32)
    c = jnp.concatenate([cos, cos, ones], axis=-1)
    s_plus = jnp.concatenate([zh, sin, zeros], axis=-1)
    s_minus = jnp.concatenate([-sin, zh, zeros], axis=-1)
    reps = V7X_LANES // head_dim
    return jnp.stack([jnp.tile(t, (1, reps)) for t in (c, s_plus, s_minus)], axis=0)


def _inproj(x, norm_w, w_in, seq):
    t, d = x.shape
    tm = TOKEN_TILE
    assert t % tm == 0 and seq % tm == 0
    n_s = seq // tm
    tab_a = _rotary_tables(seq, ROT_DIM, HEAD_DIM, ROPE_THETA)
    tab_r = _rotary_tables(seq, RET_QK_DIM, RET_QK_DIM, RET_THETA)
    widths = (ATTN_WIDTH, ATTN_WIDTH, ATTN_WIDTH, RET_QK_WIDTH, RET_QK_WIDTH, RET_WIDTH, RET_WIDTH)
    dtypes = (F32, F32, F32, BF16, BF16, BF16, F32)
    tab_spec = pl.BlockSpec((3, tm, V7X_LANES), lambda i: (0, i % n_s, 0))
    return pl.pallas_call(
        _inproj_kernel,
        grid=(t // tm,),
        in_specs=[
            pl.BlockSpec((tm, d), lambda i: (i, 0)),
            _const_spec((1, d)),
            _const_spec(w_in.shape),
            tab_spec,
            tab_spec,
        ],
        out_specs=[pl.BlockSpec((tm, w), lambda i: (i, 0)) for w in widths],
        out_shape=[jax.ShapeDtypeStruct((t, w), dt) for w, dt in zip(widths, dtypes)],
        compiler_params=_params("parallel"),
        name="inproj",
    )(x, norm_w.reshape(1, d), w_in, tab_a, tab_r)


def _attn_kernel(q_ref, k_ref, v_ref, bias_ref, o_ref,
                 qs_ref, ks_ref, vs_ref, d4_ref, acc_ref, l_ref, m_ref):
    blk = ATTN_BLK
    L = V7X_LANES
    T = ATTN_TILE
    n_pat = len(DILATIONS)
    t = pl.program_id(2)
    first_tile = (t == 0).astype(jnp.int32)

    def kv_row(d, r, m):
        return r * (T // d + blk) + blk + m

    @pl.when(t == 0)
    def _():
        for pat, d in enumerate(DILATIONS):
            for r in range(d):
                head = slice(kv_row(d, r, -blk), kv_row(d, r, 0))
                ks_ref[pat, head, :] = jnp.zeros((blk, L), BF16)
                vs_ref[pat, head, 0:L] = jnp.zeros((blk, L), BF16)
        vs_ref[:, :, L:2 * L] = jnp.ones(vs_ref.shape[:2] + (L,), BF16)

    @pl.when(t > 0)
    def _():
        for pat, d in enumerate(DILATIONS):
            for r in range(d):
                head = slice(kv_row(d, r, -blk), kv_row(d, r, 0))
                tail = slice(kv_row(d, r, T // d - blk), kv_row(d, r, T // d))
                ks_ref[pat, head, :] = ks_ref[pat, tail, :]
                vs_ref[pat, head, 0:L] = vs_ref[pat, tail, 0:L]

    def put(kind, pat, d, r, m0, x):
        n = x.shape[0]
        if kind == "q":
            qs_ref[pat, r * (T // d) + m0:r * (T // d) + m0 + n, :] = x.astype(BF16)
        elif kind == "k":
            ks_ref[pat, kv_row(d, r, m0):kv_row(d, r, m0) + n, :] = x.astype(BF16)
        else:
            vs_ref[pat, kv_row(d, r, m0):kv_row(d, r, m0) + n, 0:L] = x.astype(BF16)

    ch = 256
    n4 = T // 4
    n16 = T // 16
    for kind, ref in (("q", q_ref), ("k", k_ref), ("v", v_ref)):
        for c in range(T // ch):
            put(kind, 0, 1, 0, c * ch, ref[0, c * ch:(c + 1) * ch, :])
        for r in range(4):
            for c in range(n4 // ch):
                x = ref[0, pl.ds(r + 4 * ch * c, ch, stride=4), :]
                d4_ref[r * n4 + c * ch:r * n4 + (c + 1) * ch, :] = x
                put(kind, 1, 4, r, c * ch, x)
        for r16 in range(16):
            x = d4_ref[pl.ds((r16 % 4) * n4 + r16 // 4, n16, stride=4), :]
            put(kind, 2, 16, r16, 0, x)

    low = lax.broadcasted_iota(jnp.int32, (blk, L), 1) < HEAD_DIM
    head_a = jnp.where(low, 1.0, 0.0).astype(BF16)
    head_b = jnp.where(low, 0.0, 1.0).astype(BF16)

    def unit(pat, d, r, qb):
        n_r = T // d
        q2 = qs_ref[pat, pl.ds(_aligned(r * n_r + qb * blk, blk), blk), :]
        qq = jnp.concatenate([q2 * head_a, q2 * head_b], axis=0)
        keys = pl.ds(_aligned(kv_row(d, r, (qb - 1) * blk), blk), 2 * blk)
        k2 = ks_ref[pat, keys, :]
        v3 = vs_ref[pat, keys, :]
        no_prev = first_tile * (int(qb == 0) if isinstance(qb, int) else (qb == 0).astype(jnp.int32))
        s = lax.dot_general(qq, k2, (((1,), (1,)), ((), ())), preferred_element_type=F32) + bias_ref[no_prev]
        m = jnp.max(s, axis=-1, keepdims=True)
        p = jnp.exp2(s - m).astype(BF16)
        res = jnp.dot(p, v3, preferred_element_type=F32)
        acc = jnp.where(low, res[0:blk, 0:L], res[blk:, 0:L])
        den = jnp.where(low, res[0:blk, L:], res[blk:, L:])
        m2 = jnp.where(low, m[0:blk], m[blk:])
        if d > 1:
            pos = pl.ds(qb * blk * d + r, blk, stride=d)
            st = pat - 1
            acc_ref[st, pos, :] = acc
            l_ref[st, pos, :] = den
            m_ref[st, pos, :] = m2
        else:
            pos = pl.ds(_aligned(qb * blk, blk), blk)
            m_tot = m2
            for st in range(n_pat - 1):
                m_tot = jnp.maximum(m_tot, m_ref[st, pos, :])
            w = jnp.exp2(m2 - m_tot)
            acc = w * acc
            den = w * den
            for st in range(n_pat - 1):
                w = jnp.exp2(m_ref[st, pos, :] - m_tot)
                acc = acc + w * acc_ref[st, pos, :]
                den = den + w * l_ref[st, pos, :]
            o_ref[0, pos, :] = (acc * (1.0 / den)).astype(o_ref.dtype)

    def run(pat, d):
        n_qb = T // d // blk

        def body(g, carry):
            for j in range(ATTN_UNROLL):
                if n_qb >= ATTN_UNROLL:
                    unit(pat, d, 0, g * ATTN_UNROLL + j)
                else:
                    unit(pat, d, g * (ATTN_UNROLL // n_qb) + j // n_qb, j % n_qb)
            return carry

        lax.fori_loop(0, d * n_qb // ATTN_UNROLL, body, 0)

    for pat in reversed(range(n_pat)):
        run(pat, DILATIONS[pat])


def _attn_bias():
    blk = ATTN_BLK
    qi = jnp.arange(blk)[:, None]
    kj = jnp.arange(2 * blk)[None, :]
    dist = blk + qi - kj
    band = (dist >= 0) & (dist <= blk)
    bias = jnp.where(band, 0.0, NEG_INF).astype(F32)
    bias0 = jnp.where(band & (kj >= blk), 0.0, NEG_INF).astype(F32)
    return jnp.stack([jnp.tile(bias, (2, 1)), jnp.tile(bias0, (2, 1))], axis=0)


def _dilated_attention(q, k, v):
    b, s, w = q.shape
    T = ATTN_TILE
    L = V7X_LANES
    assert DILATIONS == (1, 4, 16) and 2 * HEAD_DIM == L and s % T == 0
    assert all(ATTN_UNROLL % (T // d // ATTN_BLK) == 0 or (d == 1 and (T // ATTN_BLK) % ATTN_UNROLL == 0)
               for d in DILATIONS)
    n_pat = len(DILATIONS)
    tile = pl.BlockSpec((1, T, L), lambda bi, p, t: (bi, t, p))
    return pl.pallas_call(
        _attn_kernel,
        grid=(b, w // L, s // T),
        in_specs=[tile, tile, tile, _const_spec((2, 2 * ATTN_BLK, 2 * ATTN_BLK))],
        out_specs=tile,
        out_shape=jax.ShapeDtypeStruct((b, s, w), BF16),
        scratch_shapes=[
            pltpu.VMEM((n_pat, T, L), BF16),
            pltpu.VMEM((n_pat, 2 * T, L), BF16),
            pltpu.VMEM((n_pat, 2 * T, 2 * L), BF16),
            pltpu.VMEM((T, L), F32),
            pltpu.VMEM((n_pat - 1, T, L), F32),
            pltpu.VMEM((n_pat - 1, T, L), F32),
            pltpu.VMEM((n_pat - 1, T, L), F32),
        ],
        compiler_params=_params("parallel", "parallel", "arbitrary"),
        name="attn",
    )(q, k, v, _attn_bias())


def _ret_kernel(q_ref, k_ref, v_ref, g_ref, hm_ref, dm_ref, qd_ref, kd_ref, cd_ref, o_ref, r_ref, *, n_chunks):
    C = RET_CHUNK
    L = V7X_LANES
    n_pairs = N_RET_HEADS // 2

    @pl.when(pl.program_id(1) == 0)
    def _():
        r_ref[...] = jnp.zeros_like(r_ref)

    state = [r_ref[p] for p in range(n_pairs)]
    for c in range(n_chunks):
        rows = slice(c * C, (c + 1) * C)
        for p in range(n_pairs):
            cols = slice(p * L, (p + 1) * L)
            q2 = q_ref[0, rows, cols]
            k2 = k_ref[0, rows, cols]
            v2 = v_ref[0, rows, 2 * p * L:2 * (p + 1) * L]
            q_st = jnp.concatenate([q2 * hm_ref[0], q2 * hm_ref[1]], axis=0)
            s = lax.dot_general(q_st, k2, (((1,), (1,)), ((), ())), preferred_element_type=F32)
            inner = (s * dm_ref[p]).astype(BF16)
            qdec = (q2.astype(F32) * qd_ref[p]).astype(BF16)
            qdec_st = jnp.concatenate([qdec * hm_ref[0], qdec * hm_ref[1]], axis=0)
            lhs = jnp.concatenate([inner, qdec_st], axis=1)
            rhs = jnp.concatenate([v2, state[p].astype(BF16)], axis=0)
            res = jnp.dot(lhs, rhs, preferred_element_type=F32)
            kdec = (k2.astype(F32) * kd_ref[p]).astype(BF16)
            kv = lax.dot_general(kdec, v2, (((0,), (0,)), ((), ())), preferred_element_type=F32)
            state[p] = cd_ref[p] * state[p] + kv
            for hh in range(2):
                h = 2 * p + hh
                o = res[hh * C:(hh + 1) * C, hh * L:(hh + 1) * L]
                rn = o * lax.rsqrt(jnp.mean(o * o, axis=-1, keepdims=True) + EPS)
                gate = g_ref[0, rows, h * RET_V_DIM:(h + 1) * RET_V_DIM]
                o_ref[0, rows, h * RET_V_DIM:(h + 1) * RET_V_DIM] = (_silu(gate) * rn).astype(o_ref.dtype)
    for p in range(n_pairs):
        r_ref[p] = state[p]


def _retention_tables():
    C = RET_CHUNK
    L = V7X_LANES
    H = N_RET_HEADS
    log_g = jnp.log1p(-(2.0 ** (-5.0 - jnp.arange(H, dtype=F32))))
    idx = jnp.arange(C, dtype=F32)
    diff = idx[:, None] - idx[None, :]
    decay_mask = jnp.where(diff[None] >= 0, jnp.exp(log_g[:, None, None] * jnp.maximum(diff, 0.0)[None]), 0.0)
    k_decay = jnp.exp(log_g[:, None] * (C - 1 - idx)[None])
    q_decay = jnp.exp(log_g[:, None] * (idx + 1.0)[None])
    chunk_decay = jnp.exp(log_g * C)
    lane_pair = lambda a: jnp.repeat(a.reshape(H // 2, 2, C), RET_QK_DIM, axis=1).transpose(0, 2, 1)
    dm = decay_mask.reshape(H // 2, 2 * C, C)
    cd = jnp.repeat(chunk_decay.reshape(H // 2, 2), RET_QK_DIM, axis=1)
    cd = jnp.broadcast_to(cd[:, :, None], (H // 2, L, 2 * L))
    lane = jnp.arange(L)
    head_mask = jnp.stack([lane < RET_QK_DIM, lane >= RET_QK_DIM]).astype(BF16)
    head_mask = jnp.broadcast_to(head_mask[:, None, :], (2, C, L))
    return head_mask, dm, lane_pair(q_decay), lane_pair(k_decay), cd


def _retention(rq, rk, rv, rg):
    b, s, _ = rq.shape
    tr = RET_TILE
    L = V7X_LANES
    assert s % tr == 0 and tr % RET_CHUNK == 0 and RET_V_DIM == L and 2 * RET_QK_DIM == L and RET_CHUNK == L
    tabs = _retention_tables()
    tile = lambda w: pl.BlockSpec((1, tr, w), lambda bi, i: (bi, i, 0))
    return pl.pallas_call(
        functools.partial(_ret_kernel, n_chunks=tr // RET_CHUNK),
        grid=(b, s // tr),
        in_specs=[tile(RET_QK_WIDTH), tile(RET_QK_WIDTH), tile(RET_WIDTH), tile(RET_WIDTH)]
        + [_const_spec(t.shape) for t in tabs],
        out_specs=tile(RET_WIDTH),
        out_shape=jax.ShapeDtypeStruct((b, s, RET_WIDTH), BF16),
        scratch_shapes=[pltpu.VMEM((N_RET_HEADS // 2, L, 2 * L), F32)],
        compiler_params=_params("parallel", "arbitrary"),
        name="retention",
    )(rq, rk, rv, rg, *tabs)


def _memkv_kernel(m_ref, nw_ref, w_ref, k_ref, v_ref):
    d = m_ref.shape[-1]
    h = _rmsnorm(m_ref[...], nw_ref[...]).astype(BF16)
    k_ref[...] = jnp.dot(h, w_ref[:, 0:d], preferred_element_type=F32).astype(BF16)
    v_ref[...] = jnp.dot(h, w_ref[:, d:], preferred_element_type=F32).astype(BF16)


def _memkv(mem, norm_w, w_ckv):
    t, d = mem.shape
    tm = min(TOKEN_TILE, t)
    assert t % tm == 0
    row = pl.BlockSpec((tm, d), lambda i: (i, 0))
    return pl.pallas_call(
        _memkv_kernel,
        grid=(t // tm,),
        in_specs=[row, _const_spec((1, d)), _const_spec(w_ckv.shape)],
        out_specs=[row, row],
        out_shape=[jax.ShapeDtypeStruct((t, d), BF16)] * 2,
        compiler_params=_params("parallel"),
        name="memkv",
    )(mem, norm_w.reshape(1, d), w_ckv)


def _cross_kernel(x_ref, oa_ref, or_ref, wmix_ref, nw_ref, wq_ref, k_ref, v_ref, wo_ref, o_ref, a_ref):
    wa = oa_ref.shape[-1]
    x = x_ref[0] + jnp.dot(oa_ref[0], wmix_ref[0:wa, :], preferred_element_type=F32)
    x = x + jnp.dot(or_ref[0], wmix_ref[wa:, :], preferred_element_type=F32)
    d = x.shape[-1]
    hd = d // N_MEM_HEADS
    h = _rmsnorm(x, nw_ref[...]).astype(BF16)
    q = (jnp.dot(h, wq_ref[...], preferred_element_type=F32) * (hd ** -0.5)).astype(BF16)
    for hh in range(N_MEM_HEADS):
        cols = slice(hh * hd, (hh + 1) * hd)
        s = lax.dot_general(q[:, cols], k_ref[0, :, cols], (((1,), (1,)), ((), ())), preferred_element_type=F32)
        m = jnp.max(s, axis=-1, keepdims=True)
        p = jnp.exp(s - m)
        l = jnp.sum(p, axis=-1, keepdims=True)
        pv = jnp.dot(p.astype(BF16), v_ref[0, :, cols], preferred_element_type=F32)
        a_ref[:, cols] = (pv * (1.0 / l)).astype(BF16)
    o_ref[0] = x + jnp.dot(a_ref[...], wo_ref[...], preferred_element_type=F32)


def _cross(x, o_attn, o_ret, w_mix, norm_w, w_cq, mem_k, mem_v, w_co):
    b, s, d = x.shape
    m = mem_k.shape[1]
    tm = TOKEN_TILE
    assert s % tm == 0
    tile = lambda w: pl.BlockSpec((1, tm, w), lambda bi, i: (bi, i, 0))
    kv = pl.BlockSpec((1, m, d), lambda bi, i: (bi, 0, 0))
    return pl.pallas_call(
        _cross_kernel,
        grid=(b, s // tm),
        in_specs=[tile(d), tile(o_attn.shape[-1]), tile(o_ret.shape[-1]), _const_spec(w_mix.shape),
                  _const_spec((1, d)), _const_spec(w_cq.shape), kv, kv, _const_spec(w_co.shape)],
        out_specs=tile(d),
        out_shape=jax.ShapeDtypeStruct((b, s, d), F32),
        scratch_shapes=[pltpu.VMEM((tm, d), BF16)],
        compiler_params=_params("parallel", "parallel"),
        name="cross",
    )(x, o_attn, o_ret, w_mix, norm_w.reshape(1, d), w_cq, mem_k, mem_v, w_co)


def kernel(x, mem, norm_ffn1, w_ffn1_in, w_ffn1_out, norm_mix, w_in, w_out, norm_cross, norm_mem,
           w_cq, w_ckv, w_co, norm_ffn2, w_ffn2_in, w_ffn2_out, norm_final):
    b, s, d = x.shape
    m = mem.shape[1]
    depth = w_in.shape[0]
    assert depth >= 1
    t = b * s
    bf = lambda a: a.astype(BF16)
    xt = x.reshape(t, d)
    for l in range(depth):
        last = l == depth - 1
        xt = _ffn(xt, norm_ffn1[l], bf(w_ffn1_in[l]), bf(w_ffn1_out[l]))
        aq, ak, av, rq, rk, rv, rg = _inproj(xt, norm_mix[l], bf(w_in[l]), s)
        sh = lambda a: a.reshape(b, s, a.shape[-1])
        o_attn = _dilated_attention(sh(aq), sh(ak), sh(av))
        o_ret = _retention(sh(rq), sh(rk), sh(rv), sh(rg))
        mk, mv = _memkv(mem.reshape(b * m, d), norm_mem[l], bf(w_ckv[l]))
        xt = _cross(xt.reshape(b, s, d), o_attn, o_ret, bf(w_out[l]), norm_cross[l], bf(w_cq[l]),
                    mk.reshape(b, m, d), mv.reshape(b, m, d), bf(w_co[l])).reshape(t, d)
        xt = _ffn(xt, norm_ffn2[l], bf(w_ffn2_in[l]), bf(w_ffn2_out[l]), norm_final if last else None)
    return xt.reshape(b, s, d)
```

```python
import functools
import math

import jax
import jax.numpy as jnp
from jax import lax
from jax.experimental import pallas as pl
from jax.experimental.pallas import tpu as pltpu

F32 = jnp.float32
BF16 = jnp.bfloat16

HEAD_DIM = 64
N_ATTN_HEADS = 8
ROT_DIM = HEAD_DIM // 4
ROPE_THETA = 500000.0
DILATIONS = (1, 4, 16)
ATTN_BLK = 128
N_RET_HEADS = 4
RET_QK_DIM = 64
RET_V_DIM = 128
RET_CHUNK = 128
RET_THETA = 10000.0
N_MEM_HEADS = 4
EPS = 1e-6
NEG_INF = -1e30
LOG2_E = 1.4426950408889634

ATTN_WIDTH = N_ATTN_HEADS * HEAD_DIM
RET_QK_WIDTH = N_RET_HEADS * RET_QK_DIM
RET_WIDTH = N_RET_HEADS * RET_V_DIM

V7X_LANES = 128
V7X_VMEM_BYTES = 64 * 1024 * 1024
VMEM_LIMIT = V7X_VMEM_BYTES - 8 * 1024 * 1024

TOKEN_TILE = 1024
ATTN_TILE = ATTN_BLK * max(DILATIONS)
ATTN_UNROLL = 16
RET_TILE = 1024


def _params(*sem):
    return pltpu.CompilerParams(dimension_semantics=sem, vmem_limit_bytes=VMEM_LIMIT)


def _const_spec(shape):
    nd = len(shape)
    return pl.BlockSpec(shape, lambda *_: (0,) * nd, pipeline_mode=pl.Buffered(1))


def _aligned(i, m):
    return i if isinstance(i, int) else pl.multiple_of(i, m)


def _rmsnorm(x, w):
    ms = jnp.mean(x * x, axis=-1, keepdims=True)
    return x * lax.rsqrt(ms + EPS) * w


def _silu(x):
    return x * jax.nn.sigmoid(x)


def _ffn_kernel(x_ref, nw_ref, win_ref, wout_ref, *rest, d_ff, chunk, final_norm):
    if final_norm:
        fw_ref, o_ref, a_ref = rest
    else:
        o_ref, a_ref = rest
    x = x_ref[...]
    h = _rmsnorm(x, nw_ref[...]).astype(BF16)
    for c in range(d_ff // chunk):
        g = jnp.dot(h, win_ref[:, c * chunk:(c + 1) * chunk], preferred_element_type=F32)
        u = jnp.dot(h, win_ref[:, d_ff + c * chunk:d_ff + (c + 1) * chunk], preferred_element_type=F32)
        a_ref[:, c * chunk:(c + 1) * chunk] = (_silu(g) * u).astype(BF16)
    y = x + 0.5 * jnp.dot(a_ref[...], wout_ref[...], preferred_element_type=F32)
    if final_norm:
        y = _rmsnorm(y, fw_ref[...])
    o_ref[...] = y


def _ffn(x, norm_w, w_in, w_out, final_w=None):
    t, d = x.shape
    d_ff = w_out.shape[0]
    tm = TOKEN_TILE
    chunk = 256
    assert t % tm == 0 and d_ff % chunk == 0
    final_norm = final_w is not None
    in_specs = [
        pl.BlockSpec((tm, d), lambda i: (i, 0)),
        _const_spec((1, d)),
        _const_spec((d, 2 * d_ff)),
        _const_spec((d_ff, d)),
    ]
    args = [x, norm_w.reshape(1, d), w_in, w_out]
    if final_norm:
        in_specs.append(_const_spec((1, d)))
        args.append(final_w.reshape(1, d))
    return pl.pallas_call(
        functools.partial(_ffn_kernel, d_ff=d_ff, chunk=chunk, final_norm=final_norm),
        grid=(t // tm,),
        in_specs=in_specs,
        out_specs=pl.BlockSpec((tm, d), lambda i: (i, 0)),
        out_shape=jax.ShapeDtypeStruct((t, d), F32),
        scratch_shapes=[pltpu.VMEM((tm, d_ff), BF16)],
        compiler_params=_params("parallel"),
        name="ffn_final" if final_norm else "ffn",
    )(*args)


def _rotate(y, tab_ref, shift):
    n = y.shape[-1]
    return (y * tab_ref[0]
            + pltpu.roll(y, shift, 1) * tab_ref[1]
            + pltpu.roll(y, n - shift, 1) * tab_ref[2])


def _inproj_kernel(x_ref, nw_ref, w_ref, ta_ref, tr_ref,
                   aq_ref, ak_ref, av_ref, rq_ref, rk_ref, rv_ref, rg_ref):
    h = _rmsnorm(x_ref[...], nw_ref[...]).astype(BF16)

    def proj(lo, width):
        return jnp.dot(h, w_ref[:, lo:lo + width], preferred_element_type=F32)

    L = V7X_LANES
    a_half = ROT_DIM // 2
    r_half = RET_QK_DIM // 2
    y = proj(0, ATTN_WIDTH)
    for c in range(ATTN_WIDTH // L):
        aq_ref[:, c * L:(c + 1) * L] = (_rotate(y[:, c * L:(c + 1) * L], ta_ref, a_half)
                                        * (HEAD_DIM ** -0.5)) * LOG2_E
    y = proj(ATTN_WIDTH, ATTN_WIDTH)
    for c in range(ATTN_WIDTH // L):
        ak_ref[:, c * L:(c + 1) * L] = _rotate(y[:, c * L:(c + 1) * L], ta_ref, a_half)
    av_ref[...] = proj(2 * ATTN_WIDTH, ATTN_WIDTH)
    base = 3 * ATTN_WIDTH
    y = proj(base, 2 * RET_QK_WIDTH)
    for c in range(RET_QK_WIDTH // L):
        rq_ref[:, c * L:(c + 1) * L] = _rotate(y[:, c * L:(c + 1) * L], tr_ref, r_half).astype(BF16)
    for c in range(RET_QK_WIDTH // L):
        yc = y[:, RET_QK_WIDTH + c * L:RET_QK_WIDTH + (c + 1) * L]
        rk_ref[:, c * L:(c + 1) * L] = (_rotate(yc, tr_ref, r_half) * (RET_QK_DIM ** -0.5)).astype(BF16)
    base += 2 * RET_QK_WIDTH
    rv_ref[...] = proj(base, RET_WIDTH).astype(BF16)
    rg_ref[...] = proj(base + RET_WIDTH, RET_WIDTH)


def _rotary_tables(seq, rot_dim, head_dim, theta):
    half = rot_dim // 2
    inv = jnp.exp(-math.log(theta) * jnp.arange(half, dtype=F32) / half)
    ang = jnp.arange(seq).astype(F32)[:, None] * inv[None, :]
    cos, sin = jnp.cos(ang), jnp.sin(ang)
    pad = head_dim - rot_dim
    ones = jnp.ones((seq, pad), F32)
    zeros = jnp.zeros((seq, pad), F32)
    zh = jnp.zeros((seq, half), F32)
    c = jnp.concatenate([cos, cos, ones], axis=-1)
    s_plus = jnp.concatenate([zh, sin, zeros], axis=-1)
    s_minus = jnp.concatenate([-sin, zh, zeros], axis=-1)
    reps = V7X_LANES // head_dim
    return jnp.stack([jnp.tile(t, (1, reps)) for t in (c, s_plus, s_minus)], axis=0)


def _inproj(x, norm_w, w_in, seq):
    t, d = x.shape
    tm = TOKEN_TILE
    assert t % tm == 0 and seq % tm == 0
    n_s = seq // tm
    tab_a = _rotary_tables(seq, ROT_DIM, HEAD_DIM, ROPE_THETA)
    tab_r = _rotary_tables(seq, RET_QK_DIM, RET_QK_DIM, RET_THETA)
    widths = (ATTN_WIDTH, ATTN_WIDTH, ATTN_WIDTH, RET_QK_WIDTH, RET_QK_WIDTH, RET_WIDTH, RET_WIDTH)
    dtypes = (F32, F32, F32, BF16, BF16, BF16, F32)
    tab_spec = pl.BlockSpec((3, tm, V7X_LANES), lambda i: (0, i % n_s, 0))
    return pl.pallas_call(
        _inproj_kernel,
        grid=(t // tm,),
        in_specs=[
            pl.BlockSpec((tm, d), lambda i: (i, 0)),
            _const_spec((1, d)),
            _const_spec(w_in.shape),
            tab_spec,
            tab_spec,
        ],
        out_specs=[pl.BlockSpec((tm, w), lambda i: (i, 0)) for w in widths],
        out_shape=[jax.ShapeDtypeStruct((t, w), dt) for w, dt in zip(widths, dtypes)],
        compiler_params=_params("parallel"),
        name="inproj",
    )(x, norm_w.reshape(1, d), w_in, tab_a, tab_r)


def _attn_kernel(q_ref, k_ref, v_ref, bias_ref, o_ref,
                 qs_ref, ks_ref, vs_ref, d4_ref, acc_ref, l_ref, m_ref):
    blk = ATTN_BLK
    L = V7X_LANES
    T = ATTN_TILE
    n_pat = len(DILATIONS)
    t = pl.program_id(2)
    first_tile = (t == 0).astype(jnp.int32)

    def kv_row(d, r, m):
        return r * (T // d + blk) + blk + m

    @pl.when(t == 0)
    def _():
        for pat, d in enumerate(DILATIONS):
            for r in range(d):
                head = slice(kv_row(d, r, -blk), kv_row(d, r, 0))
                ks_ref[pat, head, :] = jnp.zeros((blk, L), BF16)
                vs_ref[pat, head, 0:L] = jnp.zeros((blk, L), BF16)
        vs_ref[:, :, L:2 * L] = jnp.ones(vs_ref.shape[:2] + (L,), BF16)

    @pl.when(t > 0)
    def _():
        for pat, d in enumerate(DILATIONS):
            for r in range(d):
                head = slice(kv_row(d, r, -blk), kv_row(d, r, 0))
                tail = slice(kv_row(d, r, T // d - blk), kv_row(d, r, T // d))
                ks_ref[pat, head, :] = ks_ref[pat, tail, :]
                vs_ref[pat, head, 0:L] = vs_ref[pat, tail, 0:L]

    def put(kind, pat, d, r, m0, x):
        n = x.shape[0]
        if kind == "q":
            qs_ref[pat, r * (T // d) + m0:r * (T // d) + m0 + n, :] = x.astype(BF16)
        elif kind == "k":
            ks_ref[pat, kv_row(d, r, m0):kv_row(d, r, m0) + n, :] = x.astype(BF16)
        else:
            vs_ref[pat, kv_row(d, r, m0):kv_row(d, r, m0) + n, 0:L] = x.astype(BF16)

    ch = 256
    n4 = T // 4
    n16 = T // 16
    for kind, ref in (("q", q_ref), ("k", k_ref), ("v", v_ref)):
        for c in range(T // ch):
            put(kind, 0, 1, 0, c * ch, ref[0, c * ch:(c + 1) * ch, :])
        for r in range(4):
            for c in range(n4 // ch):
                x = ref[0, pl.ds(r + 4 * ch * c, ch, stride=4), :]
                d4_ref[r * n4 + c * ch:r * n4 + (c + 1) * ch, :] = x
                put(kind, 1, 4, r, c * ch, x)
        for r16 in range(16):
            x = d4_ref[pl.ds((r16 % 4) * n4 + r16 // 4, n16, stride=4), :]
            put(kind, 2, 16, r16, 0, x)

    low = lax.broadcasted_iota(jnp.int32, (blk, L), 1) < HEAD_DIM
    head_a = jnp.where(low, 1.0, 0.0).astype(BF16)
    head_b = jnp.where(low, 0.0, 1.0).astype(BF16)

    def unit(pat, d, r, qb):
        n_r = T // d
        q2 = qs_ref[pat, pl.ds(_aligned(r * n_r + qb * blk, blk), blk), :]
        qq = jnp.concatenate([q2 * head_a, q2 * head_b], axis=0)
        keys = pl.ds(_aligned(kv_row(d, r, (qb - 1) * blk), blk), 2 * blk)
        k2 = ks_ref[pat, keys, :]
        v3 = vs_ref[pat, keys, :]
        no_prev = first_tile * (int(qb == 0) if isinstance(qb, int) else (qb == 0).astype(jnp.int32))
        s = lax.dot_general(qq, k2, (((1,), (1,)), ((), ())), preferred_element_type=F32) + bias_ref[no_prev]
        m = jnp.max(s, axis=-1, keepdims=True)
        p = jnp.exp2(s - m).astype(BF16)
        res = jnp.dot(p, v3, preferred_element_type=F32)
        acc = jnp.where(low, res[0:blk, 0:L], res[blk:, 0:L])
        den = jnp.where(low, res[0:blk, L:], res[blk:, L:])
        m2 = jnp.where(low, m[0:blk], m[blk:])
        if d > 1:
            pos = pl.ds(qb * blk * d + r, blk, stride=d)
            st = pat - 1
            acc_ref[st, pos, :] = acc
            l_ref[st, pos, :] = den
            m_ref[st, pos, :] = m2
        else:
            pos = pl.ds(_aligned(qb * blk, blk), blk)
            m_tot = m2
            for st in range(n_pat - 1):
                m_tot = jnp.maximum(m_tot, m_ref[st, pos, :])
            w = jnp.exp2(m2 - m_tot)
            acc = w * acc
            den = w * den
            for st in range(n_pat - 1):
                w = jnp.exp2(m_ref[st, pos, :] - m_tot)
                acc = acc + w * acc_ref[st, pos, :]
                den = den + w * l_ref[st, pos, :]
            o_ref[0, pos, :] = (acc * (1.0 / den)).astype(o_ref.dtype)

    def run(pat, d):
        n_qb = T // d // blk

        def body(g, carry):
            for j in range(ATTN_UNROLL):
                if n_qb >= ATTN_UNROLL:
                    unit(pat, d, 0, g * ATTN_UNROLL + j)
                else:
                    unit(pat, d, g * (ATTN_UNROLL // n_qb) + j // n_qb, j % n_qb)
            return carry

        lax.fori_loop(0, d * n_qb // ATTN_UNROLL, body, 0)

    for pat in reversed(range(n_pat)):
        run(pat, DILATIONS[pat])


def _attn_bias():
    blk = ATTN_BLK
    qi = jnp.arange(blk)[:, None]
    kj = jnp.arange(2 * blk)[None, :]
    dist = blk + qi - kj
    band = (dist >= 0) & (dist <= blk)
    bias = jnp.where(band, 0.0, NEG_INF).astype(F32)
    bias0 = jnp.where(band & (kj >= blk), 0.0, NEG_INF).astype(F32)
    return jnp.stack([jnp.tile(bias, (2, 1)), jnp.tile(bias0, (2, 1))], axis=0)


def _dilated_attention(q, k, v):
    b, s, w = q.shape
    T = ATTN_TILE
    L = V7X_LANES
    assert DILATIONS == (1, 4, 16) and 2 * HEAD_DIM == L and s % T == 0
    assert all(ATTN_UNROLL % (T // d // ATTN_BLK) == 0 or (d == 1 and (T // ATTN_BLK) % ATTN_UNROLL == 0)
               for d in DILATIONS)
    n_pat = len(DILATIONS)
    tile = pl.BlockSpec((1, T, L), lambda bi, p, t: (bi, t, p))
    return pl.pallas_call(
        _attn_kernel,
        grid=(b, w // L, s // T),
        in_specs=[tile, tile, tile, _const_spec((2, 2 * ATTN_BLK, 2 * ATTN_BLK))],
        out_specs=tile,
        out_shape=jax.ShapeDtypeStruct((b, s, w), BF16),
        scratch_shapes=[
            pltpu.VMEM((n_pat, T, L), BF16),
            pltpu.VMEM((n_pat, 2 * T, L), BF16),
            pltpu.VMEM((n_pat, 2 * T, 2 * L), BF16),
            pltpu.VMEM((T, L), F32),
            pltpu.VMEM((n_pat - 1, T, L), F32),
            pltpu.VMEM((n_pat - 1, T, L), F32),
            pltpu.VMEM((n_pat - 1, T, L), F32),
        ],
        compiler_params=_params("parallel", "parallel", "arbitrary"),
        name="attn",
    )(q, k, v, _attn_bias())


def _ret_kernel(q_ref, k_ref, v_ref, g_ref, hm_ref, dm_ref, qd_ref, kd_ref, cd_ref, o_ref, r_ref, *, n_chunks):
    C = RET_CHUNK
    L = V7X_LANES
    n_pairs = N_RET_HEADS // 2

    @pl.when(pl.program_id(1) == 0)
    def _():
        r_ref[...] = jnp.zeros_like(r_ref)

    state = [r_ref[p] for p in range(n_pairs)]
    for c in range(n_chunks):
        rows = slice(c * C, (c + 1) * C)
        for p in range(n_pairs):
            cols = slice(p * L, (p + 1) * L)
            q2 = q_ref[0, rows, cols]
            k2 = k_ref[0, rows, cols]
            v2 = v_ref[0, rows, 2 * p * L:2 * (p + 1) * L]
            q_st = jnp.concatenate([q2 * hm_ref[0], q2 * hm_ref[1]], axis=0)
            s = lax.dot_general(q_st, k2, (((1,), (1,)), ((), ())), preferred_element_type=F32)
            inner = (s * dm_ref[p]).astype(BF16)
            qdec = (q2.astype(F32) * qd_ref[p]).astype(BF16)
            qdec_st = jnp.concatenate([qdec * hm_ref[0], qdec * hm_ref[1]], axis=0)
            lhs = jnp.concatenate([inner, qdec_st], axis=1)
            rhs = jnp.concatenate([v2, state[p].astype(BF16)], axis=0)
            res = jnp.dot(lhs, rhs, preferred_element_type=F32)
            kdec = (k2.astype(F32) * kd_ref[p]).astype(BF16)
            kv = lax.dot_general(kdec, v2, (((0,), (0,)), ((), ())), preferred_element_type=F32)
            state[p] = cd_ref[p] * state[p] + kv
            for hh in range(2):
                h = 2 * p + hh
                o = res[hh * C:(hh + 1) * C, hh * L:(hh + 1) * L]
                rn = o * lax.rsqrt(jnp.mean(o * o, axis=-1, keepdims=True) + EPS)
                gate = g_ref[0, rows, h * RET_V_DIM:(h + 1) * RET_V_DIM]
                o_ref[0, rows, h * RET_V_DIM:(h + 1) * RET_V_DIM] = (_silu(gate) * rn).astype(o_ref.dtype)
    for p in range(n_pairs):
        r_ref[p] = state[p]


def _retention_tables():
    C = RET_CHUNK
    L = V7X_LANES
    H = N_RET_HEADS
    log_g = jnp.log1p(-(2.0 ** (-5.0 - jnp.arange(H, dtype=F32))))
    idx = jnp.arange(C, dtype=F32)
    diff = idx[:, None] - idx[None, :]
    decay_mask = jnp.where(diff[None] >= 0, jnp.exp(log_g[:, None, None] * jnp.maximum(diff, 0.0)[None]), 0.0)
    k_decay = jnp.exp(log_g[:, None] * (C - 1 - idx)[None])
    q_decay = jnp.exp(log_g[:, None] * (idx + 1.0)[None])
    chunk_decay = jnp.exp(log_g * C)
    lane_pair = lambda a: jnp.repeat(a.reshape(H // 2, 2, C), RET_QK_DIM, axis=1).transpose(0, 2, 1)
    dm = decay_mask.reshape(H // 2, 2 * C, C)
    cd = jnp.repeat(chunk_decay.reshape(H // 2, 2), RET_QK_DIM, axis=1)
    cd = jnp.broadcast_to(cd[:, :, None], (H // 2, L, 2 * L))
    lane = jnp.arange(L)
    head_mask = jnp.stack([lane < RET_QK_DIM, lane >= RET_QK_DIM]).astype(BF16)
    head_mask = jnp.broadcast_to(head_mask[:, None, :], (2, C, L))
    return head_mask, dm, lane_pair(q_decay), lane_pair(k_decay), cd


def _retention(rq, rk, rv, rg):
    b, s, _ = rq.shape
    tr = RET_TILE
    L = V7X_LANES
    assert s % tr == 0 and tr % RET_CHUNK == 0 and RET_V_DIM == L and 2 * RET_QK_DIM == L and RET_CHUNK == L
    tabs = _retention_tables()
    tile = lambda w: pl.BlockSpec((1, tr, w), lambda bi, i: (bi, i, 0))
    return pl.pallas_call(
        functools.partial(_ret_kernel, n_chunks=tr // RET_CHUNK),
        grid=(b, s // tr),
        in_specs=[tile(RET_QK_WIDTH), tile(RET_QK_WIDTH), tile(RET_WIDTH), tile(RET_WIDTH)]
        + [_const_spec(t.shape) for t in tabs],
        out_specs=tile(RET_WIDTH),
        out_shape=jax.ShapeDtypeStruct((b, s, RET_WIDTH), BF16),
        scratch_shapes=[pltpu.VMEM((N_RET_HEADS // 2, L, 2 * L), F32)],
        compiler_params=_params("parallel", "arbitrary"),
        name="retention",
    )(rq, rk, rv, rg, *tabs)


def _memkv_kernel(m_ref, nw_ref, w_ref, k_ref, v_ref):
    d = m_ref.shape[-1]
    h = _rmsnorm(m_ref[...], nw_ref[...]).astype(BF16)
    k_ref[...] = jnp.dot(h, w_ref[:, 0:d], preferred_element_type=F32).astype(BF16)
    v_ref[...] = jnp.dot(h, w_ref[:, d:], preferred_element_type=F32).astype(BF16)


def _memkv(mem, norm_w, w_ckv):
    t, d = mem.shape
    tm = min(TOKEN_TILE, t)
    assert t % tm == 0
    row = pl.BlockSpec((tm, d), lambda i: (i, 0))
    return pl.pallas_call(
        _memkv_kernel,
        grid=(t // tm,),
        in_specs=[row, _const_spec((1, d)), _const_spec(w_ckv.shape)],
        out_specs=[row, row],
        out_shape=[jax.ShapeDtypeStruct((t, d), BF16)] * 2,
        compiler_params=_params("parallel"),
        name="memkv",
    )(mem, norm_w.reshape(1, d), w_ckv)


def _cross_kernel(x_ref, oa_ref, or_ref, wmix_ref, nw_ref, wq_ref, k_ref, v_ref, wo_ref, o_ref, a_ref):
    wa = oa_ref.shape[-1]
    x = x_ref[0] + jnp.dot(oa_ref[0], wmix_ref[0:wa, :], preferred_element_type=F32)
    x = x + jnp.dot(or_ref[0], wmix_ref[wa:, :], preferred_element_type=F32)
    d = x.shape[-1]
    hd = d // N_MEM_HEADS
    h = _rmsnorm(x, nw_ref[...]).astype(BF16)
    q = (jnp.dot(h, wq_ref[...], preferred_element_type=F32) * (hd ** -0.5)).astype(BF16)
    for hh in range(N_MEM_HEADS):
        cols = slice(hh * hd, (hh + 1) * hd)
        s = lax.dot_general(q[:, cols], k_ref[0, :, cols], (((1,), (1,)), ((), ())), preferred_element_type=F32)
        m = jnp.max(s, axis=-1, keepdims=True)
        p = jnp.exp(s - m)
        l = jnp.sum(p, axis=-1, keepdims=True)
        pv = jnp.dot(p.astype(BF16), v_ref[0, :, cols], preferred_element_type=F32)
        a_ref[:, cols] = (pv * (1.0 / l)).astype(BF16)
    o_ref[0] = x + jnp.dot(a_ref[...], wo_ref[...], preferred_element_type=F32)


def _cross(x, o_attn, o_ret, w_mix, norm_w, w_cq, mem_k, mem_v, w_co):
    b, s, d = x.shape
    m = mem_k.shape[1]
    tm = TOKEN_TILE
    assert s % tm == 0
    tile = lambda w: pl.BlockSpec((1, tm, w), lambda bi, i: (bi, i, 0))
    kv = pl.BlockSpec((1, m, d), lambda bi, i: (bi, 0, 0))
    return pl.pallas_call(
        _cross_kernel,
        grid=(b, s // tm),
        in_specs=[tile(d), tile(o_attn.shape[-1]), tile(o_ret.shape[-1]), _const_spec(w_mix.shape),
                  _const_spec((1, d)), _const_spec(w_cq.shape), kv, kv, _const_spec(w_co.shape)],
        out_specs=tile(d),
        out_shape=jax.ShapeDtypeStruct((b, s, d), F32),
        scratch_shapes=[pltpu.VMEM((tm, d), BF16)],
        compiler_params=_params("parallel", "parallel"),
        name="cross",
    )(x, o_attn, o_ret, w_mix, norm_w.reshape(1, d), w_cq, mem_k, mem_v, w_co)


def kernel(x, mem, norm_ffn1, w_ffn1_in, w_ffn1_out, norm_mix, w_in, w_out, norm_cross, norm_mem,
           w_cq, w_ckv, w_co, norm_ffn2, w_ffn2_in, w_ffn2_out, norm_final):
    b, s, d = x.shape
    m = mem.shape[1]
    depth = w_in.shape[0]
    assert depth >= 1
    t = b * s
    bf = lambda a: a.astype(BF16)
    xt = x.reshape(t, d)
    for l in range(depth):
        last = l == depth - 1
        xt = _ffn(xt, norm_ffn1[l], bf(w_ffn1_in[l]), bf(w_ffn1_out[l]))
        aq, ak, av, rq, rk, rv, rg = _inproj(xt, norm_mix[l], bf(w_in[l]), s)
        sh = lambda a: a.reshape(b, s, a.shape[-1])
        o_attn = _dilated_attention(sh(aq), sh(ak), sh(av))
        o_ret = _retention(sh(rq), sh(rk), sh(rv), sh(rg))
        mk, mv = _memkv(mem.reshape(b * m, d), norm_mem[l], bf(w_ckv[l]))
        xt = _cross(xt.reshape(b, s, d), o_attn, o_ret, bf(w_out[l]), norm_cross[l], bf(w_cq[l]),
                    mk.reshape(b, m, d), mv.reshape(b, m, d), bf(w_co[l])).reshape(t, d)
        xt = _ffn(xt, norm_ffn2[l], bf(w_ffn2_in[l]), bf(w_ffn2_out[l]), norm_final if last else None)
    return xt.reshape(b, s, d)
```

```python
import functools
import math

import jax
import jax.numpy as jnp
from jax import lax
from jax.experimental import pallas as pl
from jax.experimental.pallas import tpu as pltpu

F32 = jnp.float32
BF16 = jnp.bfloat16

HEAD_DIM = 64
N_ATTN_HEADS = 8
ROT_DIM = HEAD_DIM // 4
ROPE_THETA = 500000.0
DILATIONS = (1, 4, 16)
ATTN_BLK = 128
N_RET_HEADS = 4
RET_QK_DIM = 64
RET_V_DIM = 128
RET_CHUNK = 128
RET_THETA = 10000.0
N_MEM_HEADS = 4
EPS = 1e-6
NEG_INF = -1e30
LOG2_E = 1.4426950408889634

ATTN_WIDTH = N_ATTN_HEADS * HEAD_DIM
RET_QK_WIDTH = N_RET_HEADS * RET_QK_DIM
RET_WIDTH = N_RET_HEADS * RET_V_DIM

V7X_LANES = 128
V7X_VMEM_BYTES = 64 * 1024 * 1024
VMEM_LIMIT = V7X_VMEM_BYTES - 8 * 1024 * 1024

TOKEN_TILE = 1024
ATTN_TILE = ATTN_BLK * max(DILATIONS)
ATTN_UNROLL = 16
RET_TILE = 1024


def _params(*sem):
    return pltpu.CompilerParams(dimension_semantics=sem, vmem_limit_bytes=VMEM_LIMIT)


def _const_spec(shape):
    nd = len(shape)
    return pl.BlockSpec(shape, lambda *_: (0,) * nd, pipeline_mode=pl.Buffered(1))


def _aligned(i, m):
    return i if isinstance(i, int) else pl.multiple_of(i, m)


def _rmsnorm(x, w):
    ms = jnp.mean(x * x, axis=-1, keepdims=True)
    return x * lax.rsqrt(ms + EPS) * w


def _silu(x):
    return x * jax.nn.sigmoid(x)


def _ffn_kernel(x_ref, nw_ref, win_ref, wout_ref, *rest, d_ff, chunk, final_norm):
    if final_norm:
        fw_ref, o_ref, a_ref = rest
    else:
        o_ref, a_ref = rest
    x = x_ref[...]
    h = _rmsnorm(x, nw_ref[...]).astype(BF16)
    for c in range(d_ff // chunk):
        g = jnp.dot(h, win_ref[:, c * chunk:(c + 1) * chunk], preferred_element_type=F32)
        u = jnp.dot(h, win_ref[:, d_ff + c * chunk:d_ff + (c + 1) * chunk], preferred_element_type=F32)
        a_ref[:, c * chunk:(c + 1) * chunk] = (_silu(g) * u).astype(BF16)
    y = x + 0.5 * jnp.dot(a_ref[...], wout_ref[...], preferred_element_type=F32)
    if final_norm:
        y = _rmsnorm(y, fw_ref[...])
    o_ref[...] = y


def _ffn(x, norm_w, w_in, w_out, final_w=None):
    t, d = x.shape
    d_ff = w_out.shape[0]
    tm = TOKEN_TILE
    chunk = 256
    assert t % tm == 0 and d_ff % chunk == 0
    final_norm = final_w is not None
    in_specs = [
        pl.BlockSpec((tm, d), lambda i: (i, 0)),
        _const_spec((1, d)),
        _const_spec((d, 2 * d_ff)),
        _const_spec((d_ff, d)),
    ]
    args = [x, norm_w.reshape(1, d), w_in, w_out]
    if final_norm:
        in_specs.append(_const_spec((1, d)))
        args.append(final_w.reshape(1, d))
    return pl.pallas_call(
        functools.partial(_ffn_kernel, d_ff=d_ff, chunk=chunk, final_norm=final_norm),
        grid=(t // tm,),
        in_specs=in_specs,
        out_specs=pl.BlockSpec((tm, d), lambda i: (i, 0)),
        out_shape=jax.ShapeDtypeStruct((t, d), F32),
        scratch_shapes=[pltpu.VMEM((tm, d_ff), BF16)],
        compiler_params=_params("parallel"),
        name="ffn_final" if final_norm else "ffn",
    )(*args)


def _rotate(y, tab_ref, shift):
    n = y.shape[-1]
    return (y * tab_ref[0]
            + pltpu.roll(y, shift, 1) * tab_ref[1]
            + pltpu.roll(y, n - shift, 1) * tab_ref[2])


def _inproj_kernel(x_ref, nw_ref, w_ref, ta_ref, tr_ref,
                   aq_ref, ak_ref, av_ref, rq_ref, rk_ref, rv_ref, rg_ref):
    h = _rmsnorm(x_ref[...], nw_ref[...]).astype(BF16)

    def proj(lo, width):
        return jnp.dot(h, w_ref[:, lo:lo + width], preferred_element_type=F32)

    L = V7X_LANES
    a_half = ROT_DIM // 2
    r_half = RET_QK_DIM // 2
    y = proj(0, ATTN_WIDTH)
    for c in range(ATTN_WIDTH // L):
        aq_ref[c] = (_rotate(y[:, c * L:(c + 1) * L], ta_ref, a_half) * (HEAD_DIM ** -0.5)) * LOG2_E
    y = proj(ATTN_WIDTH, ATTN_WIDTH)
    for c in range(ATTN_WIDTH // L):
        ak_ref[c] = _rotate(y[:, c * L:(c + 1) * L], ta_ref, a_half)
    y = proj(2 * ATTN_WIDTH, ATTN_WIDTH)
    for c in range(ATTN_WIDTH // L):
        av_ref[c] = y[:, c * L:(c + 1) * L]
    base = 3 * ATTN_WIDTH
    y = proj(base, 2 * RET_QK_WIDTH)
    for c in range(RET_QK_WIDTH // L):
        rq_ref[:, c * L:(c + 1) * L] = _rotate(y[:, c * L:(c + 1) * L], tr_ref, r_half).astype(BF16)
    for c in range(RET_QK_WIDTH // L):
        yc = y[:, RET_QK_WIDTH + c * L:RET_QK_WIDTH + (c + 1) * L]
        rk_ref[:, c * L:(c + 1) * L] = (_rotate(yc, tr_ref, r_half) * (RET_QK_DIM ** -0.5)).astype(BF16)
    base += 2 * RET_QK_WIDTH
    rv_ref[...] = proj(base, RET_WIDTH).astype(BF16)
    rg_ref[...] = proj(base + RET_WIDTH, RET_WIDTH)


def _rotary_tables(seq, rot_dim, head_dim, theta):
    half = rot_dim // 2
    inv = jnp.exp(-math.log(theta) * jnp.arange(half, dtype=F32) / half)
    ang = jnp.arange(seq).astype(F32)[:, None] * inv[None, :]
    cos, sin = jnp.cos(ang), jnp.sin(ang)
    pad = head_dim - rot_dim
    ones = jnp.ones((seq, pad), F32)
    zeros = jnp.zeros((seq, pad), F32)
    zh = jnp.zeros((seq, half), F32)
    c = jnp.concatenate([cos, cos, ones], axis=-1)
    s_plus = jnp.concatenate([zh, sin, zeros], axis=-1)
    s_minus = jnp.concatenate([-sin, zh, zeros], axis=-1)
    reps = V7X_LANES // head_dim
    return jnp.stack([jnp.tile(t, (1, reps)) for t in (c, s_plus, s_minus)], axis=0)


def _inproj(x, norm_w, w_in, seq):
    t, d = x.shape
    tm = TOKEN_TILE
    assert t % tm == 0 and seq % tm == 0
    n_s = seq // tm
    tab_a = _rotary_tables(seq, ROT_DIM, HEAD_DIM, ROPE_THETA)
    tab_r = _rotary_tables(seq, RET_QK_DIM, RET_QK_DIM, RET_THETA)
    widths = (RET_QK_WIDTH, RET_QK_WIDTH, RET_WIDTH, RET_WIDTH)
    dtypes = (BF16, BF16, BF16, F32)
    n_pairs = ATTN_WIDTH // V7X_LANES
    pair_spec = pl.BlockSpec((n_pairs, tm, V7X_LANES), lambda i: (0, i, 0))
    pair_shape = jax.ShapeDtypeStruct((n_pairs, t, V7X_LANES), F32)
    tab_spec = pl.BlockSpec((3, tm, V7X_LANES), lambda i: (0, i % n_s, 0))
    return pl.pallas_call(
        _inproj_kernel,
        grid=(t // tm,),
        in_specs=[
            pl.BlockSpec((tm, d), lambda i: (i, 0)),
            _const_spec((1, d)),
            _const_spec(w_in.shape),
            tab_spec,
            tab_spec,
        ],
        out_specs=[pair_spec] * 3 + [pl.BlockSpec((tm, w), lambda i: (i, 0)) for w in widths],
        out_shape=[pair_shape] * 3 + [jax.ShapeDtypeStruct((t, w), dt) for w, dt in zip(widths, dtypes)],
        compiler_params=_params("parallel"),
        name="inproj",
    )(x, norm_w.reshape(1, d), w_in, tab_a, tab_r)


def _attn_kernel(q_ref, k_ref, v_ref, bias_ref, o_ref,
                 qs_ref, ks_ref, vs_ref, d4_ref, acc_ref, l_ref):
    blk = ATTN_BLK
    L = V7X_LANES
    T = ATTN_TILE
    n_pat = len(DILATIONS)
    t = pl.program_id(2)
    first_tile = (t == 0).astype(jnp.int32)

    def kv_row(d, r, m):
        return r * (T // d + blk) + blk + m

    @pl.when(t == 0)
    def _():
        for pat, d in enumerate(DILATIONS):
            for r in range(d):
                head = slice(kv_row(d, r, -blk), kv_row(d, r, 0))
                ks_ref[pat, head, :] = jnp.zeros((blk, L), BF16)
                vs_ref[pat, head, 0:L] = jnp.zeros((blk, L), BF16)
        vs_ref[:, :, L:2 * L] = jnp.ones(vs_ref.shape[:2] + (L,), BF16)

    @pl.when(t > 0)
    def _():
        for pat, d in enumerate(DILATIONS):
            for r in range(d):
                head = slice(kv_row(d, r, -blk), kv_row(d, r, 0))
                tail = slice(kv_row(d, r, T // d - blk), kv_row(d, r, T // d))
                ks_ref[pat, head, :] = ks_ref[pat, tail, :]
                vs_ref[pat, head, 0:L] = vs_ref[pat, tail, 0:L]

    def put(kind, pat, d, r, m0, x):
        n = x.shape[0]
        if kind == "q":
            qs_ref[pat, r * (T // d) + m0:r * (T // d) + m0 + n, :] = x.astype(BF16)
        elif kind == "k":
            ks_ref[pat, kv_row(d, r, m0):kv_row(d, r, m0) + n, :] = x.astype(BF16)
        else:
            vs_ref[pat, kv_row(d, r, m0):kv_row(d, r, m0) + n, 0:L] = x.astype(BF16)

    ch = 256
    n4 = T // 4
    n16 = T // 16
    for kind, ref in (("q", q_ref), ("k", k_ref), ("v", v_ref)):
        for c in range(T // ch):
            put(kind, 0, 1, 0, c * ch, ref[0, 0, c * ch:(c + 1) * ch, :])
        for r in range(4):
            for c in range(n4 // ch):
                x = ref[0, 0, pl.ds(r + 4 * ch * c, ch, stride=4), :]
                d4_ref[r * n4 + c * ch:r * n4 + (c + 1) * ch, :] = x
                put(kind, 1, 4, r, c * ch, x)
        for r16 in range(16):
            x = d4_ref[pl.ds((r16 % 4) * n4 + r16 // 4, n16, stride=4), :]
            put(kind, 2, 16, r16, 0, x)

    low = lax.broadcasted_iota(jnp.int32, (blk, L), 1) < HEAD_DIM
    head_a = jnp.where(low, 1.0, 0.0).astype(BF16)
    head_b = jnp.where(low, 0.0, 1.0).astype(BF16)

    def unit(pat, d, r, qb):
        n_r = T // d
        q2 = qs_ref[pat, pl.ds(_aligned(r * n_r + qb * blk, blk), blk), :]
        qq = jnp.concatenate([q2 * head_a, q2 * head_b], axis=0)
        keys = pl.ds(_aligned(kv_row(d, r, (qb - 1) * blk), blk), 2 * blk)
        k2 = ks_ref[pat, keys, :]
        v3 = vs_ref[pat, keys, :]
        no_prev = first_tile * (int(qb == 0) if isinstance(qb, int) else (qb == 0).astype(jnp.int32))
        s = lax.dot_general(qq, k2, (((1,), (1,)), ((), ())), preferred_element_type=F32) + bias_ref[no_prev]
        m = jnp.max(s, axis=-1, keepdims=True)
        p = jnp.exp2(s - m).astype(BF16)
        res = jnp.dot(p, v3, preferred_element_type=F32)
        acc = jnp.where(low, res[0:blk, 0:L], res[blk:, 0:L])
        den = jnp.where(low, res[0:blk, L:], res[blk:, L:])
        m2 = jnp.where(low, m[0:blk], m[blk:])
        if d > 1:
            pos = pl.ds(qb * blk * d + r, blk, stride=d)
            st = pat - 1
            acc_ref[st, pos, :] = acc * (1.0 / den)
            l_ref[st, pos, :] = m2 + jnp.log2(den)
        else:
            pos = pl.ds(_aligned(qb * blk, blk), blk)
            m_tot = m2
            for st in range(n_pat - 1):
                m_tot = jnp.maximum(m_tot, l_ref[st, pos, :])
            w = jnp.exp2(m2 - m_tot)
            acc = w * acc
            den = w * den
            for st in range(n_pat - 1):
                w = jnp.exp2(l_ref[st, pos, :] - m_tot)
                acc = acc + w * acc_ref[st, pos, :]
                den = den + w
            o_ref[0, 0, pos, :] = (acc * (1.0 / den)).astype(o_ref.dtype)

    def run(pat, d):
        n_qb = T // d // blk

        def body(g, carry):
            for j in range(ATTN_UNROLL):
                if n_qb >= ATTN_UNROLL:
                    unit(pat, d, 0, g * ATTN_UNROLL + j)
                else:
                    unit(pat, d, g * (ATTN_UNROLL // n_qb) + j // n_qb, j % n_qb)
            return carry

        lax.fori_loop(0, d * n_qb // ATTN_UNROLL, body, 0)

    for pat in reversed(range(n_pat)):
        run(pat, DILATIONS[pat])


def _attn_bias():
    blk = ATTN_BLK
    qi = jnp.arange(blk)[:, None]
    kj = jnp.arange(2 * blk)[None, :]
    dist = blk + qi - kj
    band = (dist >= 0) & (dist <= blk)
    bias = jnp.where(band, 0.0, NEG_INF).astype(F32)
    bias0 = jnp.where(band & (kj >= blk), 0.0, NEG_INF).astype(F32)
    return jnp.stack([jnp.tile(bias, (2, 1)), jnp.tile(bias0, (2, 1))], axis=0)


def _dilated_attention(q, k, v):
    n_pairs, b, s, L = q.shape
    T = ATTN_TILE
    assert DILATIONS == (1, 4, 16) and 2 * HEAD_DIM == L == V7X_LANES and s % T == 0
    assert all(ATTN_UNROLL % (T // d // ATTN_BLK) == 0 or (d == 1 and (T // ATTN_BLK) % ATTN_UNROLL == 0)
               for d in DILATIONS)
    n_pat = len(DILATIONS)
    tile = pl.BlockSpec((1, 1, T, L), lambda bi, p, t: (p, bi, t, 0))
    return pl.pallas_call(
        _attn_kernel,
        grid=(b, n_pairs, s // T),
        in_specs=[tile, tile, tile, _const_spec((2, 2 * ATTN_BLK, 2 * ATTN_BLK))],
        out_specs=tile,
        out_shape=jax.ShapeDtypeStruct((n_pairs, b, s, L), BF16),
        scratch_shapes=[
            pltpu.VMEM((n_pat, T, L), BF16),
            pltpu.VMEM((n_pat, 2 * T, L), BF16),
            pltpu.VMEM((n_pat, 2 * T, 2 * L), BF16),
            pltpu.VMEM((T, L), F32),
            pltpu.VMEM((n_pat - 1, T, L), F32),
            pltpu.VMEM((n_pat - 1, T, L), F32),
        ],
        compiler_params=_params("parallel", "parallel", "arbitrary"),
        name="attn",
    )(q, k, v, _attn_bias())


def _ret_kernel(q_ref, k_ref, v_ref, g_ref, hm_ref, dm_ref, qd_ref, kd_ref, cd_ref, o_ref, r_ref, *, n_chunks):
    C = RET_CHUNK
    L = V7X_LANES
    n_pairs = N_RET_HEADS // 2

    @pl.when(pl.program_id(1) == 0)
    def _():
        r_ref[...] = jnp.zeros_like(r_ref)

    state = [r_ref[p] for p in range(n_pairs)]
    for c in range(n_chunks):
        rows = slice(c * C, (c + 1) * C)
        for p in range(n_pairs):
            cols = slice(p * L, (p + 1) * L)
            q2 = q_ref[0, rows, cols]
            k2 = k_ref[0, rows, cols]
            v2 = v_ref[0, rows, 2 * p * L:2 * (p + 1) * L]
            q_st = jnp.concatenate([q2 * hm_ref[0], q2 * hm_ref[1]], axis=0)
            s = lax.dot_general(q_st, k2, (((1,), (1,)), ((), ())), preferred_element_type=F32)
            inner = (s * dm_ref[p]).astype(BF16)
            qdec = (q2.astype(F32) * qd_ref[p]).astype(BF16)
            qdec_st = jnp.concatenate([qdec * hm_ref[0], qdec * hm_ref[1]], axis=0)
            lhs = jnp.concatenate([inner, qdec_st], axis=1)
            rhs = jnp.concatenate([v2, state[p].astype(BF16)], axis=0)
            res = jnp.dot(lhs, rhs, preferred_element_type=F32)
            kdec = (k2.astype(F32) * kd_ref[p]).astype(BF16)
            kv = lax.dot_general(kdec, v2, (((0,), (0,)), ((), ())), preferred_element_type=F32)
            state[p] = cd_ref[p] * state[p] + kv
            for hh in range(2):
                h = 2 * p + hh
                o = res[hh * C:(hh + 1) * C, hh * L:(hh + 1) * L]
                rn = o * lax.rsqrt(jnp.mean(o * o, axis=-1, keepdims=True) + EPS)
                gate = g_ref[0, rows, h * RET_V_DIM:(h + 1) * RET_V_DIM]
                o_ref[0, rows, h * RET_V_DIM:(h + 1) * RET_V_DIM] = (_silu(gate) * rn).astype(o_ref.dtype)
    for p in range(n_pairs):
        r_ref[p] = state[p]


def _retention_tables():
    C = RET_CHUNK
    L = V7X_LANES
    H = N_RET_HEADS
    log_g = jnp.log1p(-(2.0 ** (-5.0 - jnp.arange(H, dtype=F32))))
    idx = jnp.arange(C, dtype=F32)
    diff = idx[:, None] - idx[None, :]
    decay_mask = jnp.where(diff[None] >= 0, jnp.exp(log_g[:, None, None] * jnp.maximum(diff, 0.0)[None]), 0.0)
    k_decay = jnp.exp(log_g[:, None] * (C - 1 - idx)[None])
    q_decay = jnp.exp(log_g[:, None] * (idx + 1.0)[None])
    chunk_decay = jnp.exp(log_g * C)
    lane_pair = lambda a: jnp.repeat(a.reshape(H // 2, 2, C), RET_QK_DIM, axis=1).transpose(0, 2, 1)
    dm = decay_mask.reshape(H // 2, 2 * C, C)
    cd = jnp.repeat(chunk_decay.reshape(H // 2, 2), RET_QK_DIM, axis=1)
    cd = jnp.broadcast_to(cd[:, :, None], (H // 2, L, 2 * L))
    lane = jnp.arange(L)
    head_mask = jnp.stack([lane < RET_QK_DIM, lane >= RET_QK_DIM]).astype(BF16)
    head_mask = jnp.broadcast_to(head_mask[:, None, :], (2, C, L))
    return head_mask, dm, lane_pair(q_decay), lane_pair(k_decay), cd


def _retention(rq, rk, rv, rg):
    b, s, _ = rq.shape
    tr = RET_TILE
    L = V7X_LANES
    assert s % tr == 0 and tr % RET_CHUNK == 0 and RET_V_DIM == L and 2 * RET_QK_DIM == L and RET_CHUNK == L
    tabs = _retention_tables()
    tile = lambda w: pl.BlockSpec((1, tr, w), lambda bi, i: (bi, i, 0))
    return pl.pallas_call(
        functools.partial(_ret_kernel, n_chunks=tr // RET_CHUNK),
        grid=(b, s // tr),
        in_specs=[tile(RET_QK_WIDTH), tile(RET_QK_WIDTH), tile(RET_WIDTH), tile(RET_WIDTH)]
        + [_const_spec(t.shape) for t in tabs],
        out_specs=tile(RET_WIDTH),
        out_shape=jax.ShapeDtypeStruct((b, s, RET_WIDTH), BF16),
        scratch_shapes=[pltpu.VMEM((N_RET_HEADS // 2, L, 2 * L), F32)],
        compiler_params=_params("parallel", "arbitrary"),
        name="retention",
    )(rq, rk, rv, rg, *tabs)


def _memkv_kernel(m_ref, nw_ref, w_ref, k_ref, v_ref):
    d = m_ref.shape[-1]
    h = _rmsnorm(m_ref[...], nw_ref[...]).astype(BF16)
    k_ref[...] = jnp.dot(h, w_ref[:, 0:d], preferred_element_type=F32).astype(BF16)
    v_ref[...] = jnp.dot(h, w_ref[:, d:], preferred_element_type=F32).astype(BF16)


def _memkv(mem, norm_w, w_ckv):
    t, d = mem.shape
    tm = min(TOKEN_TILE, t)
    assert t % tm == 0
    row = pl.BlockSpec((tm, d), lambda i: (i, 0))
    return pl.pallas_call(
        _memkv_kernel,
        grid=(t // tm,),
        in_specs=[row, _const_spec((1, d)), _const_spec(w_ckv.shape)],
        out_specs=[row, row],
        out_shape=[jax.ShapeDtypeStruct((t, d), BF16)] * 2,
        compiler_params=_params("parallel"),
        name="memkv",
    )(mem, norm_w.reshape(1, d), w_ckv)


def _cross_kernel(x_ref, oa_ref, or_ref, wmix_ref, nw_ref, wq_ref, k_ref, v_ref, wo_ref, o_ref, a_ref):
    o_attn = jnp.concatenate([oa_ref[p, 0] for p in range(oa_ref.shape[0])], axis=-1)
    wa = o_attn.shape[-1]
    x = x_ref[0] + jnp.dot(o_attn, wmix_ref[0:wa, :], preferred_element_type=F32)
    x = x + jnp.dot(or_ref[0], wmix_ref[wa:, :], preferred_element_type=F32)
    d = x.shape[-1]
    hd = d // N_MEM_HEADS
    h = _rmsnorm(x, nw_ref[...]).astype(BF16)
    q = (jnp.dot(h, wq_ref[...], preferred_element_type=F32) * (hd ** -0.5)).astype(BF16)
    for hh in range(N_MEM_HEADS):
        cols = slice(hh * hd, (hh + 1) * hd)
        s = lax.dot_general(q[:, cols], k_ref[0, :, cols], (((1,), (1,)), ((), ())), preferred_element_type=F32)
        m = jnp.max(s, axis=-1, keepdims=True)
        p = jnp.exp(s - m)
        l = jnp.sum(p, axis=-1, keepdims=True)
        pv = jnp.dot(p.astype(BF16), v_ref[0, :, cols], preferred_element_type=F32)
        a_ref[:, cols] = (pv * (1.0 / l)).astype(BF16)
    o_ref[0] = x + jnp.dot(a_ref[...], wo_ref[...], preferred_element_type=F32)


def _cross(x, o_attn, o_ret, w_mix, norm_w, w_cq, mem_k, mem_v, w_co):
    b, s, d = x.shape
    m = mem_k.shape[1]
    tm = TOKEN_TILE
    assert s % tm == 0
    tile = lambda w: pl.BlockSpec((1, tm, w), lambda bi, i: (bi, i, 0))
    kv = pl.BlockSpec((1, m, d), lambda bi, i: (bi, 0, 0))
    return pl.pallas_call(
        _cross_kernel,
        grid=(b, s // tm),
        in_specs=[tile(d), pl.BlockSpec((o_attn.shape[0], 1, tm, o_attn.shape[-1]), lambda bi, i: (0, bi, i, 0)),
                  tile(o_ret.shape[-1]), _const_spec(w_mix.shape),
                  _const_spec((1, d)), _const_spec(w_cq.shape), kv, kv, _const_spec(w_co.shape)],
        out_specs=tile(d),
        out_shape=jax.ShapeDtypeStruct((b, s, d), F32),
        scratch_shapes=[pltpu.VMEM((tm, d), BF16)],
        compiler_params=_params("parallel", "parallel"),
        name="cross",
    )(x, o_attn, o_ret, w_mix, norm_w.reshape(1, d), w_cq, mem_k, mem_v, w_co)


def kernel(x, mem, norm_ffn1, w_ffn1_in, w_ffn1_out, norm_mix, w_in, w_out, norm_cross, norm_mem,
           w_cq, w_ckv, w_co, norm_ffn2, w_ffn2_in, w_ffn2_out, norm_final):
    b, s, d = x.shape
    m = mem.shape[1]
    depth = w_in.shape[0]
    assert depth >= 1
    t = b * s
    bf = lambda a: a.astype(BF16)
    xt = x.reshape(t, d)
    for l in range(depth):
        last = l == depth - 1
        xt = _ffn(xt, norm_ffn1[l], bf(w_ffn1_in[l]), bf(w_ffn1_out[l]))
        aq, ak, av, rq, rk, rv, rg = _inproj(xt, norm_mix[l], bf(w_in[l]), s)
        sh = lambda a: a.reshape(b, s, a.shape[-1])
        pairs = lambda a: a.reshape(a.shape[0], b, s, a.shape[-1])
        o_attn = _dilated_attention(pairs(aq), pairs(ak), pairs(av))
        o_ret = _retention(sh(rq), sh(rk), sh(rv), sh(rg))
        mk, mv = _memkv(mem.reshape(b * m, d), norm_mem[l], bf(w_ckv[l]))
        xt = _cross(xt.reshape(b, s, d), o_attn, o_ret, bf(w_out[l]), norm_cross[l], bf(w_cq[l]),
                    mk.reshape(b, m, d), mv.reshape(b, m, d), bf(w_co[l])).reshape(t, d)
        xt = _ffn(xt, norm_ffn2[l], bf(w_ffn2_in[l]), bf(w_ffn2_out[l]), norm_final if last else None)
    return xt.reshape(b, s, d)
```

```python
import functools
import math

import jax
import jax.numpy as jnp
from jax import lax
from jax.experimental import pallas as pl
from jax.experimental.pallas import tpu as pltpu

F32 = jnp.float32
BF16 = jnp.bfloat16

HEAD_DIM = 64
N_ATTN_HEADS = 8
ROT_DIM = HEAD_DIM // 4
ROPE_THETA = 500000.0
DILATIONS = (1, 4, 16)
ATTN_BLK = 128
N_RET_HEADS = 4
RET_QK_DIM = 64
RET_V_DIM = 128
RET_CHUNK = 128
RET_THETA = 10000.0
N_MEM_HEADS = 4
EPS = 1e-6
NEG_INF = -1e30
LOG2_E = 1.4426950408889634

ATTN_WIDTH = N_ATTN_HEADS * HEAD_DIM
RET_QK_WIDTH = N_RET_HEADS * RET_QK_DIM
RET_WIDTH = N_RET_HEADS * RET_V_DIM

V7X_LANES = 128
V7X_VMEM_BYTES = 64 * 1024 * 1024
VMEM_LIMIT = V7X_VMEM_BYTES - 8 * 1024 * 1024

TOKEN_TILE = 1024
ATTN_TILE = ATTN_BLK * max(DILATIONS)
ATTN_UNROLL = 16
RET_TILE = 1024


def _params(*sem):
    return pltpu.CompilerParams(dimension_semantics=sem, vmem_limit_bytes=VMEM_LIMIT)


def _const_spec(shape):
    nd = len(shape)
    return pl.BlockSpec(shape, lambda *_: (0,) * nd, pipeline_mode=pl.Buffered(1))


def _aligned(i, m):
    return i if isinstance(i, int) else pl.multiple_of(i, m)


def _rmsnorm(x, w):
    ms = jnp.mean(x * x, axis=-1, keepdims=True)
    return x * lax.rsqrt(ms + EPS) * w


def _silu(x):
    return x * jax.nn.sigmoid(x)


def _ffn_kernel(x_ref, nw_ref, win_ref, wout_ref, *rest, d_ff, chunk, final_norm):
    if final_norm:
        fw_ref, o_ref, a_ref = rest
    else:
        o_ref, a_ref = rest
    x = x_ref[...]
    h = _rmsnorm(x, nw_ref[...]).astype(BF16)
    for c in range(d_ff // chunk):
        g = jnp.dot(h, win_ref[:, c * chunk:(c + 1) * chunk], preferred_element_type=F32)
        u = jnp.dot(h, win_ref[:, d_ff + c * chunk:d_ff + (c + 1) * chunk], preferred_element_type=F32)
        a_ref[:, c * chunk:(c + 1) * chunk] = (_silu(g) * u).astype(BF16)
    y = x + 0.5 * jnp.dot(a_ref[...], wout_ref[...], preferred_element_type=F32)
    if final_norm:
        y = _rmsnorm(y, fw_ref[...])
    o_ref[...] = y


def _ffn(x, norm_w, w_in, w_out, final_w=None):
    t, d = x.shape
    d_ff = w_out.shape[0]
    tm = TOKEN_TILE
    chunk = 256
    assert t % tm == 0 and d_ff % chunk == 0
    final_norm = final_w is not None
    in_specs = [
        pl.BlockSpec((tm, d), lambda i: (i, 0)),
        _const_spec((1, d)),
        _const_spec((d, 2 * d_ff)),
        _const_spec((d_ff, d)),
    ]
    args = [x, norm_w.reshape(1, d), w_in, w_out]
    if final_norm:
        in_specs.append(_const_spec((1, d)))
        args.append(final_w.reshape(1, d))
    return pl.pallas_call(
        functools.partial(_ffn_kernel, d_ff=d_ff, chunk=chunk, final_norm=final_norm),
        grid=(t // tm,),
        in_specs=in_specs,
        out_specs=pl.BlockSpec((tm, d), lambda i: (i, 0)),
        out_shape=jax.ShapeDtypeStruct((t, d), F32),
        scratch_shapes=[pltpu.VMEM((tm, d_ff), BF16)],
        compiler_params=_params("parallel"),
        name="ffn_final" if final_norm else "ffn",
    )(*args)


def _rotate(y, tab_ref, shift):
    n = y.shape[-1]
    return (y * tab_ref[0]
            + pltpu.roll(y, shift, 1) * tab_ref[1]
            + pltpu.roll(y, n - shift, 1) * tab_ref[2])


def _inproj_kernel(x_ref, nw_ref, w_ref, ta_ref, tr_ref,
                   aq_ref, ak_ref, av_ref, rq_ref, rk_ref, rv_ref, rg_ref):
    h = _rmsnorm(x_ref[...], nw_ref[...]).astype(BF16)

    def proj(lo, width):
        return jnp.dot(h, w_ref[:, lo:lo + width], preferred_element_type=F32)

    L = V7X_LANES
    a_half = ROT_DIM // 2
    r_half = RET_QK_DIM // 2
    y = proj(0, ATTN_WIDTH)
    for c in range(ATTN_WIDTH // L):
        aq_ref[c] = (_rotate(y[:, c * L:(c + 1) * L], ta_ref, a_half) * (HEAD_DIM ** -0.5)) * LOG2_E
    y = proj(ATTN_WIDTH, ATTN_WIDTH)
    for c in range(ATTN_WIDTH // L):
        ak_ref[c] = _rotate(y[:, c * L:(c + 1) * L], ta_ref, a_half)
    y = proj(2 * ATTN_WIDTH, ATTN_WIDTH)
    for c in range(ATTN_WIDTH // L):
        av_ref[c] = y[:, c * L:(c + 1) * L]
    base = 3 * ATTN_WIDTH
    y = proj(base, 2 * RET_QK_WIDTH)
    for c in range(RET_QK_WIDTH // L):
        rq_ref[:, c * L:(c + 1) * L] = _rotate(y[:, c * L:(c + 1) * L], tr_ref, r_half).astype(BF16)
    for c in range(RET_QK_WIDTH // L):
        yc = y[:, RET_QK_WIDTH + c * L:RET_QK_WIDTH + (c + 1) * L]
        rk_ref[:, c * L:(c + 1) * L] = (_rotate(yc, tr_ref, r_half) * (RET_QK_DIM ** -0.5)).astype(BF16)
    base += 2 * RET_QK_WIDTH
    rv_ref[...] = proj(base, RET_WIDTH).astype(BF16)
    rg_ref[...] = proj(base + RET_WIDTH, RET_WIDTH)


def _rotary_tables(seq, rot_dim, head_dim, theta):
    half = rot_dim // 2
    inv = jnp.exp(-math.log(theta) * jnp.arange(half, dtype=F32) / half)
    ang = jnp.arange(seq).astype(F32)[:, None] * inv[None, :]
    cos, sin = jnp.cos(ang), jnp.sin(ang)
    pad = head_dim - rot_dim
    ones = jnp.ones((seq, pad), F32)
    zeros = jnp.zeros((seq, pad), F32)
    zh = jnp.zeros((seq, half), F32)
    c = jnp.concatenate([cos, cos, ones], axis=-1)
    s_plus = jnp.concatenate([zh, sin, zeros], axis=-1)
    s_minus = jnp.concatenate([-sin, zh, zeros], axis=-1)
    reps = V7X_LANES // head_dim
    return jnp.stack([jnp.tile(t, (1, reps)) for t in (c, s_plus, s_minus)], axis=0)


def _inproj(x, norm_w, w_in, seq):
    t, d = x.shape
    tm = TOKEN_TILE
    assert t % tm == 0 and seq % tm == 0
    n_s = seq // tm
    tab_a = _rotary_tables(seq, ROT_DIM, HEAD_DIM, ROPE_THETA)
    tab_r = _rotary_tables(seq, RET_QK_DIM, RET_QK_DIM, RET_THETA)
    widths = (RET_QK_WIDTH, RET_QK_WIDTH, RET_WIDTH, RET_WIDTH)
    dtypes = (BF16, BF16, BF16, F32)
    n_pairs = ATTN_WIDTH // V7X_LANES
    pair_spec = pl.BlockSpec((n_pairs, tm, V7X_LANES), lambda i: (0, i, 0))
    pair_shape = jax.ShapeDtypeStruct((n_pairs, t, V7X_LANES), F32)
    tab_spec = pl.BlockSpec((3, tm, V7X_LANES), lambda i: (0, i % n_s, 0))
    return pl.pallas_call(
        _inproj_kernel,
        grid=(t // tm,),
        in_specs=[
            pl.BlockSpec((tm, d), lambda i: (i, 0)),
            _const_spec((1, d)),
            _const_spec(w_in.shape),
            tab_spec,
            tab_spec,
        ],
        out_specs=[pair_spec] * 3 + [pl.BlockSpec((tm, w), lambda i: (i, 0)) for w in widths],
        out_shape=[pair_shape] * 3 + [jax.ShapeDtypeStruct((t, w), dt) for w, dt in zip(widths, dtypes)],
        compiler_params=_params("parallel"),
        name="inproj",
    )(x, norm_w.reshape(1, d), w_in, tab_a, tab_r)


def _attn_kernel(q_ref, k_ref, v_ref, bias_ref, o_ref,
                 qs_ref, ks_ref, vs_ref, d4_ref, acc_ref, l_ref):
    blk = ATTN_BLK
    L = V7X_LANES
    T = ATTN_TILE
    n_pat = len(DILATIONS)
    t = pl.program_id(2)
    first_tile = (t == 0).astype(jnp.int32)

    def kv_row(d, r, m):
        return r * (T // d + blk) + blk + m

    @pl.when(t == 0)
    def _():
        for pat, d in enumerate(DILATIONS):
            for r in range(d):
                head = slice(kv_row(d, r, -blk), kv_row(d, r, 0))
                ks_ref[pat, :, head] = jnp.zeros((L, blk), BF16)
                vs_ref[pat, head, 0:L] = jnp.zeros((blk, L), BF16)
        vs_ref[:, :, L:2 * L] = jnp.ones(vs_ref.shape[:2] + (L,), BF16)

    @pl.when(t > 0)
    def _():
        for pat, d in enumerate(DILATIONS):
            for r in range(d):
                head = slice(kv_row(d, r, -blk), kv_row(d, r, 0))
                tail = slice(kv_row(d, r, T // d - blk), kv_row(d, r, T // d))
                ks_ref[pat, :, head] = ks_ref[pat, :, tail]
                vs_ref[pat, head, 0:L] = vs_ref[pat, tail, 0:L]

    def put(kind, pat, d, r, m0, x):
        n = x.shape[0]
        if kind == "q":
            qs_ref[pat, r * (T // d) + m0:r * (T // d) + m0 + n, :] = x.astype(BF16)
        elif kind == "k":
            ks_ref[pat, :, kv_row(d, r, m0):kv_row(d, r, m0) + n] = x.T.astype(BF16)
        else:
            vs_ref[pat, kv_row(d, r, m0):kv_row(d, r, m0) + n, 0:L] = x.astype(BF16)

    ch = 256
    n4 = T // 4
    n16 = T // 16
    for kind, ref in (("q", q_ref), ("k", k_ref), ("v", v_ref)):
        for c in range(T // ch):
            put(kind, 0, 1, 0, c * ch, ref[0, 0, c * ch:(c + 1) * ch, :])
        for r in range(4):
            for c in range(n4 // ch):
                x = ref[0, 0, pl.ds(r + 4 * ch * c, ch, stride=4), :]
                d4_ref[r * n4 + c * ch:r * n4 + (c + 1) * ch, :] = x
                put(kind, 1, 4, r, c * ch, x)
        for r16 in range(16):
            x = d4_ref[pl.ds((r16 % 4) * n4 + r16 // 4, n16, stride=4), :]
            put(kind, 2, 16, r16, 0, x)

    low = lax.broadcasted_iota(jnp.int32, (blk, L), 1) < HEAD_DIM
    head_a = jnp.where(low, 1.0, 0.0).astype(BF16)
    head_b = jnp.where(low, 0.0, 1.0).astype(BF16)
    own_row = (lax.broadcasted_iota(jnp.int32, (2 * blk, blk), 0) % blk
               == lax.broadcasted_iota(jnp.int32, (2 * blk, blk), 1))
    own_row = jnp.where(own_row, 1.0, 0.0).astype(BF16)

    def unit(pat, d, r, qb):
        n_r = T // d
        q2 = qs_ref[pat, pl.ds(_aligned(r * n_r + qb * blk, blk), blk), :]
        qq = jnp.concatenate([q2 * head_a, q2 * head_b], axis=0)
        keys = pl.ds(_aligned(kv_row(d, r, (qb - 1) * blk), blk), 2 * blk)
        v3 = vs_ref[pat, keys, :]
        no_prev = first_tile * (int(qb == 0) if isinstance(qb, int) else (qb == 0).astype(jnp.int32))
        lhs = jnp.concatenate([qq, own_row], axis=1)
        rhs = jnp.concatenate([ks_ref[pat, :, keys], bias_ref[no_prev]], axis=0)
        s = jnp.dot(lhs, rhs, preferred_element_type=F32)
        m = jnp.max(s, axis=-1, keepdims=True)
        p = jnp.exp2(s - m).astype(BF16)
        res = jnp.dot(p, v3, preferred_element_type=F32)
        acc = jnp.where(low, res[0:blk, 0:L], res[blk:, 0:L])
        den = jnp.where(low, res[0:blk, L:], res[blk:, L:])
        m2 = jnp.where(low, m[0:blk], m[blk:])
        if d > 1:
            pos = pl.ds(qb * blk * d + r, blk, stride=d)
            st = pat - 1
            acc_ref[st, pos, :] = acc * (1.0 / den)
            l_ref[st, pos, :] = m2 + jnp.log2(den)
        else:
            pos = pl.ds(_aligned(qb * blk, blk), blk)
            m_tot = m2
            for st in range(n_pat - 1):
                m_tot = jnp.maximum(m_tot, l_ref[st, pos, :])
            w = jnp.exp2(m2 - m_tot)
            acc = w * acc
            den = w * den
            for st in range(n_pat - 1):
                w = jnp.exp2(l_ref[st, pos, :] - m_tot)
                acc = acc + w * acc_ref[st, pos, :]
                den = den + w
            o_ref[0, 0, pos, :] = (acc * (1.0 / den)).astype(o_ref.dtype)

    def run(pat, d):
        n_qb = T // d // blk

        def body(g, carry):
            for j in range(ATTN_UNROLL):
                if n_qb >= ATTN_UNROLL:
                    unit(pat, d, 0, g * ATTN_UNROLL + j)
                else:
                    unit(pat, d, g * (ATTN_UNROLL // n_qb) + j // n_qb, j % n_qb)
            return carry

        lax.fori_loop(0, d * n_qb // ATTN_UNROLL, body, 0)

    for pat in reversed(range(n_pat)):
        run(pat, DILATIONS[pat])


def _attn_bias():
    blk = ATTN_BLK
    qi = jnp.arange(blk)[:, None]
    kj = jnp.arange(2 * blk)[None, :]
    dist = blk + qi - kj
    band = (dist >= 0) & (dist <= blk)
    bias = jnp.where(band, 0.0, NEG_INF).astype(F32)
    bias0 = jnp.where(band & (kj >= blk), 0.0, NEG_INF).astype(F32)
    return jnp.stack([bias, bias0], axis=0).astype(BF16)


def _dilated_attention(q, k, v):
    n_pairs, b, s, L = q.shape
    T = ATTN_TILE
    assert DILATIONS == (1, 4, 16) and 2 * HEAD_DIM == L == V7X_LANES and s % T == 0
    assert all(ATTN_UNROLL % (T // d // ATTN_BLK) == 0 or (d == 1 and (T // ATTN_BLK) % ATTN_UNROLL == 0)
               for d in DILATIONS)
    n_pat = len(DILATIONS)
    tile = pl.BlockSpec((1, 1, T, L), lambda bi, p, t: (p, bi, t, 0))
    return pl.pallas_call(
        _attn_kernel,
        grid=(b, n_pairs, s // T),
        in_specs=[tile, tile, tile, _const_spec((2, ATTN_BLK, 2 * ATTN_BLK))],
        out_specs=tile,
        out_shape=jax.ShapeDtypeStruct((n_pairs, b, s, L), BF16),
        scratch_shapes=[
            pltpu.VMEM((n_pat, T, L), BF16),
            pltpu.VMEM((n_pat, L, 2 * T), BF16),
            pltpu.VMEM((n_pat, 2 * T, 2 * L), BF16),
            pltpu.VMEM((T, L), F32),
            pltpu.VMEM((n_pat - 1, T, L), F32),
            pltpu.VMEM((n_pat - 1, T, L), F32),
        ],
        compiler_params=_params("parallel", "parallel", "arbitrary"),
        name="attn",
    )(q, k, v, _attn_bias())


def _ret_kernel(q_ref, k_ref, v_ref, g_ref, hm_ref, dm_ref, qd_ref, kd_ref, cd_ref, o_ref, r_ref, *, n_chunks):
    C = RET_CHUNK
    L = V7X_LANES
    n_pairs = N_RET_HEADS // 2

    @pl.when(pl.program_id(1) == 0)
    def _():
        r_ref[...] = jnp.zeros_like(r_ref)

    state = [r_ref[p] for p in range(n_pairs)]
    for c in range(n_chunks):
        rows = slice(c * C, (c + 1) * C)
        for p in range(n_pairs):
            cols = slice(p * L, (p + 1) * L)
            q2 = q_ref[0, rows, cols]
            k2 = k_ref[0, rows, cols]
            v2 = v_ref[0, rows, 2 * p * L:2 * (p + 1) * L]
            q_st = jnp.concatenate([q2 * hm_ref[0], q2 * hm_ref[1]], axis=0)
            s = lax.dot_general(q_st, k2, (((1,), (1,)), ((), ())), preferred_element_type=F32)
            inner = (s * dm_ref[p]).astype(BF16)
            qdec = (q2.astype(F32) * qd_ref[p]).astype(BF16)
            qdec_st = jnp.concatenate([qdec * hm_ref[0], qdec * hm_ref[1]], axis=0)
            lhs = jnp.concatenate([inner, qdec_st], axis=1)
            rhs = jnp.concatenate([v2, state[p].astype(BF16)], axis=0)
            res = jnp.dot(lhs, rhs, preferred_element_type=F32)
            kdec = (k2.astype(F32) * kd_ref[p]).astype(BF16)
            kv = lax.dot_general(kdec, v2, (((0,), (0,)), ((), ())), preferred_element_type=F32)
            state[p] = cd_ref[p] * state[p] + kv
            for hh in range(2):
                h = 2 * p + hh
                o = res[hh * C:(hh + 1) * C, hh * L:(hh + 1) * L]
                rn = o * lax.rsqrt(jnp.mean(o * o, axis=-1, keepdims=True) + EPS)
                gate = g_ref[0, rows, h * RET_V_DIM:(h + 1) * RET_V_DIM]
                o_ref[0, rows, h * RET_V_DIM:(h + 1) * RET_V_DIM] = (_silu(gate) * rn).astype(o_ref.dtype)
    for p in range(n_pairs):
        r_ref[p] = state[p]


def _retention_tables():
    C = RET_CHUNK
    L = V7X_LANES
    H = N_RET_HEADS
    log_g = jnp.log1p(-(2.0 ** (-5.0 - jnp.arange(H, dtype=F32))))
    idx = jnp.arange(C, dtype=F32)
    diff = idx[:, None] - idx[None, :]
    decay_mask = jnp.where(diff[None] >= 0, jnp.exp(log_g[:, None, None] * jnp.maximum(diff, 0.0)[None]), 0.0)
    k_decay = jnp.exp(log_g[:, None] * (C - 1 - idx)[None])
    q_decay = jnp.exp(log_g[:, None] * (idx + 1.0)[None])
    chunk_decay = jnp.exp(log_g * C)
    lane_pair = lambda a: jnp.repeat(a.reshape(H // 2, 2, C), RET_QK_DIM, axis=1).transpose(0, 2, 1)
    dm = decay_mask.reshape(H // 2, 2 * C, C)
    cd = jnp.repeat(chunk_decay.reshape(H // 2, 2), RET_QK_DIM, axis=1)
    cd = jnp.broadcast_to(cd[:, :, None], (H // 2, L, 2 * L))
    lane = jnp.arange(L)
    head_mask = jnp.stack([lane < RET_QK_DIM, lane >= RET_QK_DIM]).astype(BF16)
    head_mask = jnp.broadcast_to(head_mask[:, None, :], (2, C, L))
    return head_mask, dm, lane_pair(q_decay), lane_pair(k_decay), cd


def _retention(rq, rk, rv, rg):
    b, s, _ = rq.shape
    tr = RET_TILE
    L = V7X_LANES
    assert s % tr == 0 and tr % RET_CHUNK == 0 and RET_V_DIM == L and 2 * RET_QK_DIM == L and RET_CHUNK == L
    tabs = _retention_tables()
    tile = lambda w: pl.BlockSpec((1, tr, w), lambda bi, i: (bi, i, 0))
    return pl.pallas_call(
        functools.partial(_ret_kernel, n_chunks=tr // RET_CHUNK),
        grid=(b, s // tr),
        in_specs=[tile(RET_QK_WIDTH), tile(RET_QK_WIDTH), tile(RET_WIDTH), tile(RET_WIDTH)]
        + [_const_spec(t.shape) for t in tabs],
        out_specs=tile(RET_WIDTH),
        out_shape=jax.ShapeDtypeStruct((b, s, RET_WIDTH), BF16),
        scratch_shapes=[pltpu.VMEM((N_RET_HEADS // 2, L, 2 * L), F32)],
        compiler_params=_params("parallel", "arbitrary"),
        name="retention",
    )(rq, rk, rv, rg, *tabs)


def _memkv_kernel(m_ref, nw_ref, w_ref, k_ref, v_ref):
    d = m_ref.shape[-1]
    h = _rmsnorm(m_ref[...], nw_ref[...]).astype(BF16)
    k_ref[...] = jnp.dot(h, w_ref[:, 0:d], preferred_element_type=F32).astype(BF16)
    v_ref[...] = jnp.dot(h, w_ref[:, d:], preferred_element_type=F32).astype(BF16)


def _memkv(mem, norm_w, w_ckv):
    t, d = mem.shape
    tm = min(TOKEN_TILE, t)
    assert t % tm == 0
    row = pl.BlockSpec((tm, d), lambda i: (i, 0))
    return pl.pallas_call(
        _memkv_kernel,
        grid=(t // tm,),
        in_specs=[row, _const_spec((1, d)), _const_spec(w_ckv.shape)],
        out_specs=[row, row],
        out_shape=[jax.ShapeDtypeStruct((t, d), BF16)] * 2,
        compiler_params=_params("parallel"),
        name="memkv",
    )(mem, norm_w.reshape(1, d), w_ckv)


def _cross_kernel(x_ref, oa_ref, or_ref, wmix_ref, nw_ref, wq_ref, k_ref, v_ref, wo_ref, o_ref, a_ref):
    o_attn = jnp.concatenate([oa_ref[p, 0] for p in range(oa_ref.shape[0])], axis=-1)
    wa = o_attn.shape[-1]
    x = x_ref[0] + jnp.dot(o_attn, wmix_ref[0:wa, :], preferred_element_type=F32)
    x = x + jnp.dot(or_ref[0], wmix_ref[wa:, :], preferred_element_type=F32)
    d = x.shape[-1]
    hd = d // N_MEM_HEADS
    h = _rmsnorm(x, nw_ref[...]).astype(BF16)
    q = (jnp.dot(h, wq_ref[...], preferred_element_type=F32) * (hd ** -0.5)).astype(BF16)
    for hh in range(N_MEM_HEADS):
        cols = slice(hh * hd, (hh + 1) * hd)
        s = lax.dot_general(q[:, cols], k_ref[0, :, cols], (((1,), (1,)), ((), ())), preferred_element_type=F32)
        m = jnp.max(s, axis=-1, keepdims=True)
        p = jnp.exp(s - m)
        l = jnp.sum(p, axis=-1, keepdims=True)
        pv = jnp.dot(p.astype(BF16), v_ref[0, :, cols], preferred_element_type=F32)
        a_ref[:, cols] = (pv * (1.0 / l)).astype(BF16)
    o_ref[0] = x + jnp.dot(a_ref[...], wo_ref[...], preferred_element_type=F32)


def _cross(x, o_attn, o_ret, w_mix, norm_w, w_cq, mem_k, mem_v, w_co):
    b, s, d = x.shape
    m = mem_k.shape[1]
    tm = TOKEN_TILE
    assert s % tm == 0
    tile = lambda w: pl.BlockSpec((1, tm, w), lambda bi, i: (bi, i, 0))
    kv = pl.BlockSpec((1, m, d), lambda bi, i: (bi, 0, 0))
    return pl.pallas_call(
        _cross_kernel,
        grid=(b, s // tm),
        in_specs=[tile(d), pl.BlockSpec((o_attn.shape[0], 1, tm, o_attn.shape[-1]), lambda bi, i: (0, bi, i, 0)),
                  tile(o_ret.shape[-1]), _const_spec(w_mix.shape),
                  _const_spec((1, d)), _const_spec(w_cq.shape), kv, kv, _const_spec(w_co.shape)],
        out_specs=tile(d),
        out_shape=jax.ShapeDtypeStruct((b, s, d), F32),
        scratch_shapes=[pltpu.VMEM((tm, d), BF16)],
        compiler_params=_params("parallel", "parallel"),
        name="cross",
    )(x, o_attn, o_ret, w_mix, norm_w.reshape(1, d), w_cq, mem_k, mem_v, w_co)


def kernel(x, mem, norm_ffn1, w_ffn1_in, w_ffn1_out, norm_mix, w_in, w_out, norm_cross, norm_mem,
           w_cq, w_ckv, w_co, norm_ffn2, w_ffn2_in, w_ffn2_out, norm_final):
    b, s, d = x.shape
    m = mem.shape[1]
    depth = w_in.shape[0]
    assert depth >= 1
    t = b * s
    bf = lambda a: a.astype(BF16)
    xt = x.reshape(t, d)
    for l in range(depth):
        last = l == depth - 1
        xt = _ffn(xt, norm_ffn1[l], bf(w_ffn1_in[l]), bf(w_ffn1_out[l]))
        aq, ak, av, rq, rk, rv, rg = _inproj(xt, norm_mix[l], bf(w_in[l]), s)
        sh = lambda a: a.reshape(b, s, a.shape[-1])
        pairs = lambda a: a.reshape(a.shape[0], b, s, a.shape[-1])
        o_attn = _dilated_attention(pairs(aq), pairs(ak), pairs(av))
        o_ret = _retention(sh(rq), sh(rk), sh(rv), sh(rg))
        mk, mv = _memkv(mem.reshape(b * m, d), norm_mem[l], bf(w_ckv[l]))
        xt = _cross(xt.reshape(b, s, d), o_attn, o_ret, bf(w_out[l]), norm_cross[l], bf(w_cq[l]),
                    mk.reshape(b, m, d), mv.reshape(b, m, d), bf(w_co[l])).reshape(t, d)
        xt = _ffn(xt, norm_ffn2[l], bf(w_ffn2_in[l]), bf(w_ffn2_out[l]), norm_final if last else None)
    return xt.reshape(b, s, d)
```

```python
import functools
import math

import jax
import jax.numpy as jnp
import numpy as np
from jax import lax
from jax.experimental import pallas as pl
from jax.experimental.pallas import tpu as pltpu

F32 = jnp.float32
BF16 = jnp.bfloat16

HEAD_DIM = 64
N_ATTN_HEADS = 8
ROT_DIM = HEAD_DIM // 4
ROPE_THETA = 500000.0
DILATIONS = (1, 4, 16)
ATTN_BLK = 128
N_RET_HEADS = 4
RET_QK_DIM = 64
RET_V_DIM = 128
RET_CHUNK = 128
RET_THETA = 10000.0
N_MEM_HEADS = 4
EPS = 1e-6
NEG_INF = -1e30
LOG2_E = 1.4426950408889634

ATTN_WIDTH = N_ATTN_HEADS * HEAD_DIM
RET_QK_WIDTH = N_RET_HEADS * RET_QK_DIM
RET_WIDTH = N_RET_HEADS * RET_V_DIM

V7X_LANES = 128
V7X_VMEM_BYTES = 64 * 1024 * 1024
VMEM_LIMIT = V7X_VMEM_BYTES - 8 * 1024 * 1024

TOKEN_TILE = 1024
ATTN_TILE = ATTN_BLK * max(DILATIONS)
ATTN_UNROLL = 16
RET_TILE = 2048


def _params(*sem):
    return pltpu.CompilerParams(dimension_semantics=sem, vmem_limit_bytes=VMEM_LIMIT)


def _const_spec(shape):
    nd = len(shape)
    return pl.BlockSpec(shape, lambda *_: (0,) * nd, pipeline_mode=pl.Buffered(1))


def _aligned(i, m):
    return i if isinstance(i, int) else pl.multiple_of(i, m)


def _rmsnorm(x, w):
    ms = jnp.mean(x * x, axis=-1, keepdims=True)
    return x * lax.rsqrt(ms + EPS) * w


def _silu(x):
    return x * jax.nn.sigmoid(x)


def _ffn_kernel(x_ref, nw_ref, win_ref, wout_ref, *rest, d_ff, chunk, final_norm):
    if final_norm:
        fw_ref, o_ref, a_ref = rest
    else:
        o_ref, a_ref = rest
    x = x_ref[...]
    h = _rmsnorm(x, nw_ref[...]).astype(BF16)
    for c in range(d_ff // chunk):
        g = jnp.dot(h, win_ref[:, c * chunk:(c + 1) * chunk], preferred_element_type=F32)
        u = jnp.dot(h, win_ref[:, d_ff + c * chunk:d_ff + (c + 1) * chunk], preferred_element_type=F32)
        a_ref[:, c * chunk:(c + 1) * chunk] = (_silu(g) * u).astype(BF16)
    y = x + 0.5 * jnp.dot(a_ref[...], wout_ref[...], preferred_element_type=F32)
    if final_norm:
        y = _rmsnorm(y, fw_ref[...])
    o_ref[...] = y


def _ffn(x, norm_w, w_in, w_out, final_w=None):
    t, d = x.shape
    d_ff = w_out.shape[0]
    tm = TOKEN_TILE
    chunk = 256
    assert t % tm == 0 and d_ff % chunk == 0
    final_norm = final_w is not None
    in_specs = [
        pl.BlockSpec((tm, d), lambda i: (i, 0)),
        _const_spec((1, d)),
        _const_spec((d, 2 * d_ff)),
        _const_spec((d_ff, d)),
    ]
    args = [x, norm_w.reshape(1, d), w_in, w_out]
    if final_norm:
        in_specs.append(_const_spec((1, d)))
        args.append(final_w.reshape(1, d))
    return pl.pallas_call(
        functools.partial(_ffn_kernel, d_ff=d_ff, chunk=chunk, final_norm=final_norm),
        grid=(t // tm,),
        in_specs=in_specs,
        out_specs=pl.BlockSpec((tm, d), lambda i: (i, 0)),
        out_shape=jax.ShapeDtypeStruct((t, d), F32),
        scratch_shapes=[pltpu.VMEM((tm, d_ff), BF16)],
        compiler_params=_params("parallel"),
        name="ffn_final" if final_norm else "ffn",
    )(*args)


def _rotate(y, tab_ref, shift):
    n = y.shape[-1]
    return (y * tab_ref[0]
            + pltpu.roll(y, shift, 1) * tab_ref[1]
            + pltpu.roll(y, n - shift, 1) * tab_ref[2])


def _inproj_kernel(x_ref, nw_ref, w_ref, ta_ref, tr_ref,
                   aq_ref, ak_ref, av_ref, rq_ref, rk_ref, rv_ref, rg_ref):
    h = _rmsnorm(x_ref[...], nw_ref[...]).astype(BF16)

    def proj(lo, width):
        return jnp.dot(h, w_ref[:, lo:lo + width], preferred_element_type=F32)

    L = V7X_LANES
    a_half = ROT_DIM // 2
    r_half = RET_QK_DIM // 2
    y = proj(0, ATTN_WIDTH)
    for c in range(ATTN_WIDTH // L):
        aq_ref[c] = (_rotate(y[:, c * L:(c + 1) * L], ta_ref, a_half) * (HEAD_DIM ** -0.5)) * LOG2_E
    y = proj(ATTN_WIDTH, ATTN_WIDTH)
    for c in range(ATTN_WIDTH // L):
        ak_ref[c] = _rotate(y[:, c * L:(c + 1) * L], ta_ref, a_half)
    y = proj(2 * ATTN_WIDTH, ATTN_WIDTH)
    for c in range(ATTN_WIDTH // L):
        av_ref[c] = y[:, c * L:(c + 1) * L]
    base = 3 * ATTN_WIDTH
    y = proj(base, 2 * RET_QK_WIDTH)
    for c in range(RET_QK_WIDTH // L):
        rq_ref[:, c * L:(c + 1) * L] = _rotate(y[:, c * L:(c + 1) * L], tr_ref, r_half).astype(BF16)
    for c in range(RET_QK_WIDTH // L):
        yc = y[:, RET_QK_WIDTH + c * L:RET_QK_WIDTH + (c + 1) * L]
        rk_ref[:, c * L:(c + 1) * L] = (_rotate(yc, tr_ref, r_half) * (RET_QK_DIM ** -0.5)).astype(BF16)
    base += 2 * RET_QK_WIDTH
    rv_ref[...] = proj(base, RET_WIDTH).astype(BF16)
    rg_ref[...] = proj(base + RET_WIDTH, RET_WIDTH)


def _rotary_tables(seq, rot_dim, head_dim, theta):
    half = rot_dim // 2
    inv = np.exp(-math.log(theta) * np.arange(half, dtype=np.float64) / half)
    ang = np.arange(seq, dtype=np.float64)[:, None] * inv[None, :]
    cos, sin = np.cos(ang), np.sin(ang)
    pad = head_dim - rot_dim
    ones = np.ones((seq, pad))
    zeros = np.zeros((seq, pad))
    zh = np.zeros((seq, half))
    c = np.concatenate([cos, cos, ones], axis=-1)
    s_plus = np.concatenate([zh, sin, zeros], axis=-1)
    s_minus = np.concatenate([-sin, zh, zeros], axis=-1)
    reps = V7X_LANES // head_dim
    return np.stack([np.tile(t, (1, reps)) for t in (c, s_plus, s_minus)], axis=0).astype(np.float32)


def _inproj(x, norm_w, w_in, seq):
    t, d = x.shape
    tm = TOKEN_TILE
    assert t % tm == 0 and seq % tm == 0
    n_s = seq // tm
    tab_a = _rotary_tables(seq, ROT_DIM, HEAD_DIM, ROPE_THETA)
    tab_r = _rotary_tables(seq, RET_QK_DIM, RET_QK_DIM, RET_THETA)
    widths = (RET_QK_WIDTH, RET_QK_WIDTH, RET_WIDTH, RET_WIDTH)
    dtypes = (BF16, BF16, BF16, F32)
    n_pairs = ATTN_WIDTH // V7X_LANES
    pair_spec = pl.BlockSpec((n_pairs, tm, V7X_LANES), lambda i: (0, i, 0))
    pair_shape = jax.ShapeDtypeStruct((n_pairs, t, V7X_LANES), F32)
    tab_spec = pl.BlockSpec((3, tm, V7X_LANES), lambda i: (0, i % n_s, 0))
    return pl.pallas_call(
        _inproj_kernel,
        grid=(t // tm,),
        in_specs=[
            pl.BlockSpec((tm, d), lambda i: (i, 0)),
            _const_spec((1, d)),
            _const_spec(w_in.shape),
            tab_spec,
            tab_spec,
        ],
        out_specs=[pair_spec] * 3 + [pl.BlockSpec((tm, w), lambda i: (i, 0)) for w in widths],
        out_shape=[pair_shape] * 3 + [jax.ShapeDtypeStruct((t, w), dt) for w, dt in zip(widths, dtypes)],
        compiler_params=_params("parallel"),
        name="inproj",
    )(x, norm_w.reshape(1, d), w_in, tab_a, tab_r)


def _attn_kernel(q_ref, k_ref, v_ref, bias_ref, o_ref,
                 qs_ref, ks_ref, vs_ref, d4_ref, acc_ref, l_ref):
    blk = ATTN_BLK
    L = V7X_LANES
    T = ATTN_TILE
    n_pat = len(DILATIONS)
    t = pl.program_id(2)
    first_tile = (t == 0).astype(jnp.int32)

    def kv_row(d, r, m):
        return r * (T // d + blk) + blk + m

    @pl.when(t == 0)
    def _():
        for pat, d in enumerate(DILATIONS):
            for r in range(d):
                head = slice(kv_row(d, r, -blk), kv_row(d, r, 0))
                ks_ref[pat, head, :] = jnp.zeros((blk, L), BF16)
                vs_ref[pat, head, 0:L] = jnp.zeros((blk, L), BF16)
        vs_ref[:, :, L:2 * L] = jnp.ones(vs_ref.shape[:2] + (L,), BF16)

    @pl.when(t > 0)
    def _():
        for pat, d in enumerate(DILATIONS):
            for r in range(d):
                head = slice(kv_row(d, r, -blk), kv_row(d, r, 0))
                tail = slice(kv_row(d, r, T // d - blk), kv_row(d, r, T // d))
                ks_ref[pat, head, :] = ks_ref[pat, tail, :]
                vs_ref[pat, head, 0:L] = vs_ref[pat, tail, 0:L]

    def put(kind, pat, d, r, m0, x):
        n = x.shape[0]
        if kind == "q":
            qs_ref[pat, r * (T // d) + m0:r * (T // d) + m0 + n, :] = x.astype(BF16)
        elif kind == "k":
            ks_ref[pat, kv_row(d, r, m0):kv_row(d, r, m0) + n, :] = x.astype(BF16)
        else:
            vs_ref[pat, kv_row(d, r, m0):kv_row(d, r, m0) + n, 0:L] = x.astype(BF16)

    ch = 256
    n4 = T // 4
    n16 = T // 16
    for kind, ref in (("q", q_ref), ("k", k_ref), ("v", v_ref)):
        for c in range(T // ch):
            put(kind, 0, 1, 0, c * ch, ref[0, 0, c * ch:(c + 1) * ch, :])
        for r in range(4):
            for c in range(n4 // ch):
                x = ref[0, 0, pl.ds(r + 4 * ch * c, ch, stride=4), :]
                d4_ref[r * n4 + c * ch:r * n4 + (c + 1) * ch, :] = x
                put(kind, 1, 4, r, c * ch, x)
        for r16 in range(16):
            x = d4_ref[pl.ds((r16 % 4) * n4 + r16 // 4, n16, stride=4), :]
            put(kind, 2, 16, r16, 0, x)

    low = lax.broadcasted_iota(jnp.int32, (blk, L), 1) < HEAD_DIM
    head_a = jnp.where(low, 1.0, 0.0).astype(BF16)
    head_b = jnp.where(low, 0.0, 1.0).astype(BF16)

    def unit(pat, d, r, qb):
        n_r = T // d
        q2 = qs_ref[pat, pl.ds(_aligned(r * n_r + qb * blk, blk), blk), :]
        qq = jnp.concatenate([q2 * head_a, q2 * head_b], axis=0)
        keys = pl.ds(_aligned(kv_row(d, r, (qb - 1) * blk), blk), 2 * blk)
        k2 = ks_ref[pat, keys, :]
        v3 = vs_ref[pat, keys, :]
        no_prev = first_tile * (int(qb == 0) if isinstance(qb, int) else (qb == 0).astype(jnp.int32))
        s = lax.dot_general(qq, k2, (((1,), (1,)), ((), ())), preferred_element_type=F32) + bias_ref[no_prev]
        m = jnp.max(s, axis=-1, keepdims=True)
        p = jnp.exp2(s - m).astype(BF16)
        res = jnp.dot(p, v3, preferred_element_type=F32)
        acc = jnp.where(low, res[0:blk, 0:L], res[blk:, 0:L])
        den = jnp.where(low, res[0:blk, L:], res[blk:, L:])
        m2 = jnp.where(low, m[0:blk], m[blk:])
        if d > 1:
            pos = pl.ds(qb * blk * d + r, blk, stride=d)
            st = pat - 1
            acc_ref[st, pos, :] = acc * (1.0 / den)
            l_ref[st, pos, :] = m2 + jnp.log2(den)
        else:
            pos = pl.ds(_aligned(qb * blk, blk), blk)
            m_tot = m2
            for st in range(n_pat - 1):
                m_tot = jnp.maximum(m_tot, l_ref[st, pos, :])
            w = jnp.exp2(m2 - m_tot)
            acc = w * acc
            den = w * den
            for st in range(n_pat - 1):
                w = jnp.exp2(l_ref[st, pos, :] - m_tot)
                acc = acc + w * acc_ref[st, pos, :]
                den = den + w
            o_ref[0, 0, pos, :] = (acc * (1.0 / den)).astype(o_ref.dtype)

    def run(pat, d):
        n_qb = T // d // blk

        def body(g, carry):
            for j in range(ATTN_UNROLL):
                if n_qb >= ATTN_UNROLL:
                    unit(pat, d, 0, g * ATTN_UNROLL + j)
                else:
                    unit(pat, d, g * (ATTN_UNROLL // n_qb) + j // n_qb, j % n_qb)
            return carry

        lax.fori_loop(0, d * n_qb // ATTN_UNROLL, body, 0)

    for pat in reversed(range(n_pat)):
        run(pat, DILATIONS[pat])


def _attn_bias():
    blk = ATTN_BLK
    qi = np.arange(blk)[:, None]
    kj = np.arange(2 * blk)[None, :]
    dist = blk + qi - kj
    band = (dist >= 0) & (dist <= blk)
    bias = np.where(band, 0.0, NEG_INF)
    bias0 = np.where(band & (kj >= blk), 0.0, NEG_INF)
    return np.stack([np.tile(bias, (2, 1)), np.tile(bias0, (2, 1))], axis=0).astype(np.float32)


def _dilated_attention(q, k, v):
    n_pairs, b, s, L = q.shape
    T = ATTN_TILE
    assert DILATIONS == (1, 4, 16) and 2 * HEAD_DIM == L == V7X_LANES and s % T == 0
    assert all(ATTN_UNROLL % (T // d // ATTN_BLK) == 0 or (d == 1 and (T // ATTN_BLK) % ATTN_UNROLL == 0)
               for d in DILATIONS)
    n_pat = len(DILATIONS)
    tile = pl.BlockSpec((1, 1, T, L), lambda bi, p, t: (p, bi, t, 0))
    return pl.pallas_call(
        _attn_kernel,
        grid=(b, n_pairs, s // T),
        in_specs=[tile, tile, tile, _const_spec((2, 2 * ATTN_BLK, 2 * ATTN_BLK))],
        out_specs=tile,
        out_shape=jax.ShapeDtypeStruct((n_pairs, b, s, L), BF16),
        scratch_shapes=[
            pltpu.VMEM((n_pat, T, L), BF16),
            pltpu.VMEM((n_pat, 2 * T, L), BF16),
            pltpu.VMEM((n_pat, 2 * T, 2 * L), BF16),
            pltpu.VMEM((T, L), F32),
            pltpu.VMEM((n_pat - 1, T, L), F32),
            pltpu.VMEM((n_pat - 1, T, L), F32),
        ],
        compiler_params=_params("parallel", "parallel", "arbitrary"),
        name="attn",
    )(q, k, v, _attn_bias())


def _ret_kernel(q_ref, k_ref, v_ref, g_ref, hm_ref, dm_ref, qd_ref, kd_ref, cd_ref, o_ref, r_ref, *, n_chunks):
    C = RET_CHUNK
    L = V7X_LANES
    n_pairs = N_RET_HEADS // 2

    @pl.when(pl.program_id(1) == 0)
    def _():
        r_ref[...] = jnp.zeros_like(r_ref)

    state = [r_ref[p] for p in range(n_pairs)]
    for c in range(n_chunks):
        rows = slice(c * C, (c + 1) * C)
        for p in range(n_pairs):
            cols = slice(p * L, (p + 1) * L)
            q2 = q_ref[0, rows, cols]
            k2 = k_ref[0, rows, cols]
            v2 = v_ref[0, rows, 2 * p * L:2 * (p + 1) * L]
            q_st = jnp.concatenate([q2 * hm_ref[0], q2 * hm_ref[1]], axis=0)
            s = lax.dot_general(q_st, k2, (((1,), (1,)), ((), ())), preferred_element_type=F32)
            inner = (s * dm_ref[p]).astype(BF16)
            qdec = (q2.astype(F32) * qd_ref[p]).astype(BF16)
            qdec_st = jnp.concatenate([qdec * hm_ref[0], qdec * hm_ref[1]], axis=0)
            lhs = jnp.concatenate([inner, qdec_st], axis=1)
            rhs = jnp.concatenate([v2, state[p].astype(BF16)], axis=0)
            res = jnp.dot(lhs, rhs, preferred_element_type=F32)
            kdec = (k2.astype(F32) * kd_ref[p]).astype(BF16)
            kv = lax.dot_general(kdec, v2, (((0,), (0,)), ((), ())), preferred_element_type=F32)
            state[p] = cd_ref[p] * state[p] + kv
            for hh in range(2):
                h = 2 * p + hh
                o = res[hh * C:(hh + 1) * C, hh * L:(hh + 1) * L]
                rn = o * lax.rsqrt(jnp.mean(o * o, axis=-1, keepdims=True) + EPS)
                gate = g_ref[0, rows, h * RET_V_DIM:(h + 1) * RET_V_DIM]
                o_ref[0, rows, h * RET_V_DIM:(h + 1) * RET_V_DIM] = (_silu(gate) * rn).astype(o_ref.dtype)
    for p in range(n_pairs):
        r_ref[p] = state[p]


def _retention_tables():
    C = RET_CHUNK
    L = V7X_LANES
    H = N_RET_HEADS
    log_g = np.log1p(-(2.0 ** (-5.0 - np.arange(H, dtype=np.float64))))
    idx = np.arange(C, dtype=np.float64)
    diff = idx[:, None] - idx[None, :]
    decay_mask = np.where(diff[None] >= 0, np.exp(log_g[:, None, None] * np.maximum(diff, 0.0)[None]), 0.0)
    k_decay = np.exp(log_g[:, None] * (C - 1 - idx)[None])
    q_decay = np.exp(log_g[:, None] * (idx + 1.0)[None])
    chunk_decay = np.exp(log_g * C)
    lane_pair = lambda a: np.repeat(a.reshape(H // 2, 2, C), RET_QK_DIM, axis=1).transpose(0, 2, 1)
    dm = decay_mask.reshape(H // 2, 2 * C, C)
    cd = np.repeat(chunk_decay.reshape(H // 2, 2), RET_QK_DIM, axis=1)
    cd = np.broadcast_to(cd[:, :, None], (H // 2, L, 2 * L))
    lane = np.arange(L)
    head_mask = np.stack([lane < RET_QK_DIM, lane >= RET_QK_DIM]).astype(np.float32)
    head_mask = np.broadcast_to(head_mask[:, None, :], (2, C, L))
    f32 = lambda a: np.ascontiguousarray(a, dtype=np.float32)
    return jnp.asarray(f32(head_mask), BF16), f32(dm), f32(lane_pair(q_decay)), f32(lane_pair(k_decay)), f32(cd)


def _retention(rq, rk, rv, rg):
    b, s, _ = rq.shape
    tr = RET_TILE
    L = V7X_LANES
    assert s % tr == 0 and tr % RET_CHUNK == 0 and RET_V_DIM == L and 2 * RET_QK_DIM == L and RET_CHUNK == L
    tabs = _retention_tables()
    tile = lambda w: pl.BlockSpec((1, tr, w), lambda bi, i: (bi, i, 0))
    return pl.pallas_call(
        functools.partial(_ret_kernel, n_chunks=tr // RET_CHUNK),
        grid=(b, s // tr),
        in_specs=[tile(RET_QK_WIDTH), tile(RET_QK_WIDTH), tile(RET_WIDTH), tile(RET_WIDTH)]
        + [_const_spec(t.shape) for t in tabs],
        out_specs=tile(RET_WIDTH),
        out_shape=jax.ShapeDtypeStruct((b, s, RET_WIDTH), BF16),
        scratch_shapes=[pltpu.VMEM((N_RET_HEADS // 2, L, 2 * L), F32)],
        compiler_params=_params("parallel", "arbitrary"),
        name="retention",
    )(rq, rk, rv, rg, *tabs)


def _memkv_kernel(m_ref, nw_ref, w_ref, k_ref, v_ref):
    d = m_ref.shape[-1]
    h = _rmsnorm(m_ref[...], nw_ref[...]).astype(BF16)
    k_ref[...] = jnp.dot(h, w_ref[:, 0:d], preferred_element_type=F32).astype(BF16)
    v_ref[...] = jnp.dot(h, w_ref[:, d:], preferred_element_type=F32).astype(BF16)


def _memkv(mem, norm_w, w_ckv):
    t, d = mem.shape
    tm = min(TOKEN_TILE, t)
    assert t % tm == 0
    row = pl.BlockSpec((tm, d), lambda i: (i, 0))
    return pl.pallas_call(
        _memkv_kernel,
        grid=(t // tm,),
        in_specs=[row, _const_spec((1, d)), _const_spec(w_ckv.shape)],
        out_specs=[row, row],
        out_shape=[jax.ShapeDtypeStruct((t, d), BF16)] * 2,
        compiler_params=_params("parallel"),
        name="memkv",
    )(mem, norm_w.reshape(1, d), w_ckv)


def _cross_kernel(x_ref, oa_ref, or_ref, wmix_ref, nw_ref, wq_ref, k_ref, v_ref, wo_ref, o_ref, a_ref):
    o_attn = jnp.concatenate([oa_ref[p, 0] for p in range(oa_ref.shape[0])], axis=-1)
    wa = o_attn.shape[-1]
    x = x_ref[0] + jnp.dot(o_attn, wmix_ref[0:wa, :], preferred_element_type=F32)
    x = x + jnp.dot(or_ref[0], wmix_ref[wa:, :], preferred_element_type=F32)
    d = x.shape[-1]
    hd = d // N_MEM_HEADS
    h = _rmsnorm(x, nw_ref[...]).astype(BF16)
    q = (jnp.dot(h, wq_ref[...], preferred_element_type=F32) * (hd ** -0.5)).astype(BF16)
    for hh in range(N_MEM_HEADS):
        cols = slice(hh * hd, (hh + 1) * hd)
        s = lax.dot_general(q[:, cols], k_ref[0, :, cols], (((1,), (1,)), ((), ())), preferred_element_type=F32)
        m = jnp.max(s, axis=-1, keepdims=True)
        p = jnp.exp(s - m)
        l = jnp.sum(p, axis=-1, keepdims=True)
        pv = jnp.dot(p.astype(BF16), v_ref[0, :, cols], preferred_element_type=F32)
        a_ref[:, cols] = (pv * (1.0 / l)).astype(BF16)
    o_ref[0] = x + jnp.dot(a_ref[...], wo_ref[...], preferred_element_type=F32)


def _cross(x, o_attn, o_ret, w_mix, norm_w, w_cq, mem_k, mem_v, w_co):
    b, s, d = x.shape
    m = mem_k.shape[1]
    tm = TOKEN_TILE
    assert s % tm == 0
    tile = lambda w: pl.BlockSpec((1, tm, w), lambda bi, i: (bi, i, 0))
    kv = pl.BlockSpec((1, m, d), lambda bi, i: (bi, 0, 0))
    return pl.pallas_call(
        _cross_kernel,
        grid=(b, s // tm),
        in_specs=[tile(d), pl.BlockSpec((o_attn.shape[0], 1, tm, o_attn.shape[-1]), lambda bi, i: (0, bi, i, 0)),
                  tile(o_ret.shape[-1]), _const_spec(w_mix.shape),
                  _const_spec((1, d)), _const_spec(w_cq.shape), kv, kv, _const_spec(w_co.shape)],
        out_specs=tile(d),
        out_shape=jax.ShapeDtypeStruct((b, s, d), F32),
        scratch_shapes=[pltpu.VMEM((tm, d), BF16)],
        compiler_params=_params("parallel", "parallel"),
        name="cross",
    )(x, o_attn, o_ret, w_mix, norm_w.reshape(1, d), w_cq, mem_k, mem_v, w_co)


def kernel(x, mem, norm_ffn1, w_ffn1_in, w_ffn1_out, norm_mix, w_in, w_out, norm_cross, norm_mem,
           w_cq, w_ckv, w_co, norm_ffn2, w_ffn2_in, w_ffn2_out, norm_final):
    b, s, d = x.shape
    m = mem.shape[1]
    depth = w_in.shape[0]
    assert depth >= 1
    t = b * s
    bf = lambda a: a.astype(BF16)
    xt = x.reshape(t, d)
    for l in range(depth):
        last = l == depth - 1
        xt = _ffn(xt, norm_ffn1[l], bf(w_ffn1_in[l]), bf(w_ffn1_out[l]))
        aq, ak, av, rq, rk, rv, rg = _inproj(xt, norm_mix[l], bf(w_in[l]), s)
        sh = lambda a: a.reshape(b, s, a.shape[-1])
        pairs = lambda a: a.reshape(a.shape[0], b, s, a.shape[-1])
        o_attn = _dilated_attention(pairs(aq), pairs(ak), pairs(av))
        o_ret = _retention(sh(rq), sh(rk), sh(rv), sh(rg))
        mk, mv = _memkv(mem.reshape(b * m, d), norm_mem[l], bf(w_ckv[l]))
        xt = _cross(xt.reshape(b, s, d), o_attn, o_ret, bf(w_out[l]), norm_cross[l], bf(w_cq[l]),
                    mk.reshape(b, m, d), mv.reshape(b, m, d), bf(w_co[l])).reshape(t, d)
        xt = _ffn(xt, norm_ffn2[l], bf(w_ffn2_in[l]), bf(w_ffn2_out[l]), norm_final if last else None)
    return xt.reshape(b, s, d)
```

```python
import functools
import math

import jax
import jax.numpy as jnp
import numpy as np
from jax import lax
from jax.experimental import pallas as pl
from jax.experimental.pallas import tpu as pltpu

F32 = jnp.float32
BF16 = jnp.bfloat16

HEAD_DIM = 64
N_ATTN_HEADS = 8
ROT_DIM = HEAD_DIM // 4
ROPE_THETA = 500000.0
DILATIONS = (1, 4, 16)
ATTN_BLK = 128
N_RET_HEADS = 4
RET_QK_DIM = 64
RET_V_DIM = 128
RET_CHUNK = 128
RET_THETA = 10000.0
N_MEM_HEADS = 4
EPS = 1e-6
NEG_INF = -1e30
LOG2_E = 1.4426950408889634

ATTN_WIDTH = N_ATTN_HEADS * HEAD_DIM
RET_QK_WIDTH = N_RET_HEADS * RET_QK_DIM
RET_WIDTH = N_RET_HEADS * RET_V_DIM

V7X_LANES = 128
V7X_VMEM_BYTES = 64 * 1024 * 1024
VMEM_LIMIT = V7X_VMEM_BYTES - 8 * 1024 * 1024

TOKEN_TILE = 1024
ATTN_TILE = 2 * ATTN_BLK * max(DILATIONS)
ATTN_UNROLL = 32
RET_TILE = 2048


def _params(*sem):
    return pltpu.CompilerParams(dimension_semantics=sem, vmem_limit_bytes=VMEM_LIMIT)


def _const_spec(shape):
    nd = len(shape)
    return pl.BlockSpec(shape, lambda *_: (0,) * nd, pipeline_mode=pl.Buffered(1))


def _aligned(i, m):
    return i if isinstance(i, int) else pl.multiple_of(i, m)


def _rmsnorm(x, w):
    ms = jnp.mean(x * x, axis=-1, keepdims=True)
    return x * lax.rsqrt(ms + EPS) * w


def _silu(x):
    return x * jax.nn.sigmoid(x)


def _ffn_kernel(x_ref, nw_ref, win_ref, wout_ref, *rest, d_ff, chunk, final_norm):
    if final_norm:
        fw_ref, o_ref, a_ref = rest
    else:
        o_ref, a_ref = rest
    x = x_ref[...]
    h = _rmsnorm(x, nw_ref[...]).astype(BF16)
    for c in range(d_ff // chunk):
        g = jnp.dot(h, win_ref[:, c * chunk:(c + 1) * chunk], preferred_element_type=F32)
        u = jnp.dot(h, win_ref[:, d_ff + c * chunk:d_ff + (c + 1) * chunk], preferred_element_type=F32)
        a_ref[:, c * chunk:(c + 1) * chunk] = (_silu(g) * u).astype(BF16)
    y = x + 0.5 * jnp.dot(a_ref[...], wout_ref[...], preferred_element_type=F32)
    if final_norm:
        y = _rmsnorm(y, fw_ref[...])
    o_ref[...] = y


def _ffn(x, norm_w, w_in, w_out, final_w=None):
    t, d = x.shape
    d_ff = w_out.shape[0]
    tm = TOKEN_TILE
    chunk = 256
    assert t % tm == 0 and d_ff % chunk == 0
    final_norm = final_w is not None
    in_specs = [
        pl.BlockSpec((tm, d), lambda i: (i, 0)),
        _const_spec((1, d)),
        _const_spec((d, 2 * d_ff)),
        _const_spec((d_ff, d)),
    ]
    args = [x, norm_w.reshape(1, d), w_in, w_out]
    if final_norm:
        in_specs.append(_const_spec((1, d)))
        args.append(final_w.reshape(1, d))
    return pl.pallas_call(
        functools.partial(_ffn_kernel, d_ff=d_ff, chunk=chunk, final_norm=final_norm),
        grid=(t // tm,),
        in_specs=in_specs,
        out_specs=pl.BlockSpec((tm, d), lambda i: (i, 0)),
        out_shape=jax.ShapeDtypeStruct((t, d), F32),
        scratch_shapes=[pltpu.VMEM((tm, d_ff), BF16)],
        compiler_params=_params("parallel"),
        name="ffn_final" if final_norm else "ffn",
    )(*args)


def _rotate(y, tab_ref, shift):
    n = y.shape[-1]
    return (y * tab_ref[0]
            + pltpu.roll(y, shift, 1) * tab_ref[1]
            + pltpu.roll(y, n - shift, 1) * tab_ref[2])


def _inproj_kernel(x_ref, nw_ref, w_ref, ta_ref, tr_ref,
                   aq_ref, ak_ref, av_ref, rq_ref, rk_ref, rv_ref, rg_ref):
    h = _rmsnorm(x_ref[...], nw_ref[...]).astype(BF16)

    def proj(lo, width):
        return jnp.dot(h, w_ref[:, lo:lo + width], preferred_element_type=F32)

    L = V7X_LANES
    a_half = ROT_DIM // 2
    r_half = RET_QK_DIM // 2
    y = proj(0, ATTN_WIDTH)
    for c in range(ATTN_WIDTH // L):
        aq_ref[c] = (_rotate(y[:, c * L:(c + 1) * L], ta_ref, a_half) * (HEAD_DIM ** -0.5)) * LOG2_E
    y = proj(ATTN_WIDTH, ATTN_WIDTH)
    for c in range(ATTN_WIDTH // L):
        ak_ref[c] = _rotate(y[:, c * L:(c + 1) * L], ta_ref, a_half)
    y = proj(2 * ATTN_WIDTH, ATTN_WIDTH)
    for c in range(ATTN_WIDTH // L):
        av_ref[c] = y[:, c * L:(c + 1) * L]
    base = 3 * ATTN_WIDTH
    y = proj(base, 2 * RET_QK_WIDTH)
    for c in range(RET_QK_WIDTH // L):
        rq_ref[:, c * L:(c + 1) * L] = _rotate(y[:, c * L:(c + 1) * L], tr_ref, r_half).astype(BF16)
    for c in range(RET_QK_WIDTH // L):
        yc = y[:, RET_QK_WIDTH + c * L:RET_QK_WIDTH + (c + 1) * L]
        rk_ref[:, c * L:(c + 1) * L] = (_rotate(yc, tr_ref, r_half) * (RET_QK_DIM ** -0.5)).astype(BF16)
    base += 2 * RET_QK_WIDTH
    rv_ref[...] = proj(base, RET_WIDTH).astype(BF16)
    rg_ref[...] = proj(base + RET_WIDTH, RET_WIDTH)


def _rotary_tables(seq, rot_dim, head_dim, theta):
    half = rot_dim // 2
    inv = np.exp(-math.log(theta) * np.arange(half, dtype=np.float64) / half)
    ang = np.arange(seq, dtype=np.float64)[:, None] * inv[None, :]
    cos, sin = np.cos(ang), np.sin(ang)
    pad = head_dim - rot_dim
    ones = np.ones((seq, pad))
    zeros = np.zeros((seq, pad))
    zh = np.zeros((seq, half))
    c = np.concatenate([cos, cos, ones], axis=-1)
    s_plus = np.concatenate([zh, sin, zeros], axis=-1)
    s_minus = np.concatenate([-sin, zh, zeros], axis=-1)
    reps = V7X_LANES // head_dim
    return np.stack([np.tile(t, (1, reps)) for t in (c, s_plus, s_minus)], axis=0).astype(np.float32)


def _inproj(x, norm_w, w_in, seq):
    t, d = x.shape
    tm = TOKEN_TILE
    assert t % tm == 0 and seq % tm == 0
    n_s = seq // tm
    tab_a = _rotary_tables(seq, ROT_DIM, HEAD_DIM, ROPE_THETA)
    tab_r = _rotary_tables(seq, RET_QK_DIM, RET_QK_DIM, RET_THETA)
    widths = (RET_QK_WIDTH, RET_QK_WIDTH, RET_WIDTH, RET_WIDTH)
    dtypes = (BF16, BF16, BF16, F32)
    n_pairs = ATTN_WIDTH // V7X_LANES
    pair_spec = pl.BlockSpec((n_pairs, tm, V7X_LANES), lambda i: (0, i, 0))
    pair_shape = jax.ShapeDtypeStruct((n_pairs, t, V7X_LANES), F32)
    tab_spec = pl.BlockSpec((3, tm, V7X_LANES), lambda i: (0, i % n_s, 0))
    return pl.pallas_call(
        _inproj_kernel,
        grid=(t // tm,),
        in_specs=[
            pl.BlockSpec((tm, d), lambda i: (i, 0)),
            _const_spec((1, d)),
            _const_spec(w_in.shape),
            tab_spec,
            tab_spec,
        ],
        out_specs=[pair_spec] * 3 + [pl.BlockSpec((tm, w), lambda i: (i, 0)) for w in widths],
        out_shape=[pair_shape] * 3 + [jax.ShapeDtypeStruct((t, w), dt) for w, dt in zip(widths, dtypes)],
        compiler_params=_params("parallel"),
        name="inproj",
    )(x, norm_w.reshape(1, d), w_in, tab_a, tab_r)


def _attn_kernel(q_ref, k_ref, v_ref, bias_ref, o_ref,
                 qs_ref, ks_ref, vs_ref, d4_ref, acc_ref, l_ref):
    blk = ATTN_BLK
    L = V7X_LANES
    T = ATTN_TILE
    n_pat = len(DILATIONS)
    t = pl.program_id(2)
    first_tile = (t == 0).astype(jnp.int32)

    def kv_row(d, r, m):
        return r * (T // d + blk) + blk + m

    @pl.when(t == 0)
    def _():
        for pat, d in enumerate(DILATIONS):
            for r in range(d):
                head = slice(kv_row(d, r, -blk), kv_row(d, r, 0))
                ks_ref[pat, head, :] = jnp.zeros((blk, L), BF16)
                vs_ref[pat, head, 0:L] = jnp.zeros((blk, L), BF16)
        vs_ref[:, :, L:2 * L] = jnp.ones(vs_ref.shape[:2] + (L,), BF16)

    @pl.when(t > 0)
    def _():
        for pat, d in enumerate(DILATIONS):
            for r in range(d):
                head = slice(kv_row(d, r, -blk), kv_row(d, r, 0))
                tail = slice(kv_row(d, r, T // d - blk), kv_row(d, r, T // d))
                ks_ref[pat, head, :] = ks_ref[pat, tail, :]
                vs_ref[pat, head, 0:L] = vs_ref[pat, tail, 0:L]

    def put(kind, pat, d, r, m0, x):
        n = x.shape[0]
        if kind == "q":
            qs_ref[pat, r * (T // d) + m0:r * (T // d) + m0 + n, :] = x.astype(BF16)
        elif kind == "k":
            ks_ref[pat, kv_row(d, r, m0):kv_row(d, r, m0) + n, :] = x.astype(BF16)
        else:
            vs_ref[pat, kv_row(d, r, m0):kv_row(d, r, m0) + n, 0:L] = x.astype(BF16)

    ch = 256
    n4 = T // 4
    n16 = T // 16
    for kind, ref in (("q", q_ref), ("k", k_ref), ("v", v_ref)):
        for c in range(T // ch):
            put(kind, 0, 1, 0, c * ch, ref[0, 0, c * ch:(c + 1) * ch, :])
        for r in range(4):
            for c in range(n4 // ch):
                x = ref[0, 0, pl.ds(r + 4 * ch * c, ch, stride=4), :]
                d4_ref[r * n4 + c * ch:r * n4 + (c + 1) * ch, :] = x
                put(kind, 1, 4, r, c * ch, x)
        for r16 in range(16):
            x = d4_ref[pl.ds((r16 % 4) * n4 + r16 // 4, n16, stride=4), :]
            put(kind, 2, 16, r16, 0, x)

    low = lax.broadcasted_iota(jnp.int32, (blk, L), 1) < HEAD_DIM
    head_a = jnp.where(low, 1.0, 0.0).astype(BF16)
    head_b = jnp.where(low, 0.0, 1.0).astype(BF16)

    def unit(pat, d, r, qb):
        n_r = T // d
        q2 = qs_ref[pat, pl.ds(_aligned(r * n_r + qb * blk, blk), blk), :]
        qq = jnp.concatenate([q2 * head_a, q2 * head_b], axis=0)
        keys = pl.ds(_aligned(kv_row(d, r, (qb - 1) * blk), blk), 2 * blk)
        k2 = ks_ref[pat, keys, :]
        v3 = vs_ref[pat, keys, :]
        no_prev = first_tile * (int(qb == 0) if isinstance(qb, int) else (qb == 0).astype(jnp.int32))
        s = lax.dot_general(qq, k2, (((1,), (1,)), ((), ())), preferred_element_type=F32) + bias_ref[no_prev]
        m = jnp.max(s, axis=-1, keepdims=True)
        p = jnp.exp2(s - m).astype(BF16)
        res = jnp.dot(p, v3, preferred_element_type=F32)
        acc = jnp.where(low, res[0:blk, 0:L], res[blk:, 0:L])
        den = jnp.where(low, res[0:blk, L:], res[blk:, L:])
        m2 = jnp.where(low, m[0:blk], m[blk:])
        if d > 1:
            pos = pl.ds(qb * blk * d + r, blk, stride=d)
            st = pat - 1
            acc_ref[st, pos, :] = acc * (1.0 / den)
            l_ref[st, pos, :] = m2 + jnp.log2(den)
        else:
            pos = pl.ds(_aligned(qb * blk, blk), blk)
            m_tot = m2
            for st in range(n_pat - 1):
                m_tot = jnp.maximum(m_tot, l_ref[st, pos, :])
            w = jnp.exp2(m2 - m_tot)
            acc = w * acc
            den = w * den
            for st in range(n_pat - 1):
                w = jnp.exp2(l_ref[st, pos, :] - m_tot)
                acc = acc + w * acc_ref[st, pos, :]
                den = den + w
            o_ref[0, 0, pos, :] = (acc * (1.0 / den)).astype(o_ref.dtype)

    def run(pat, d):
        n_qb = T // d // blk

        def body(g, carry):
            for j in range(ATTN_UNROLL):
                if n_qb >= ATTN_UNROLL:
                    unit(pat, d, 0, g * ATTN_UNROLL + j)
                else:
                    unit(pat, d, g * (ATTN_UNROLL // n_qb) + j // n_qb, j % n_qb)
            return carry

        lax.fori_loop(0, d * n_qb // ATTN_UNROLL, body, 0)

    for pat in reversed(range(n_pat)):
        run(pat, DILATIONS[pat])


def _attn_bias():
    blk = ATTN_BLK
    qi = np.arange(blk)[:, None]
    kj = np.arange(2 * blk)[None, :]
    dist = blk + qi - kj
    band = (dist >= 0) & (dist <= blk)
    bias = np.where(band, 0.0, NEG_INF)
    bias0 = np.where(band & (kj >= blk), 0.0, NEG_INF)
    return np.stack([np.tile(bias, (2, 1)), np.tile(bias0, (2, 1))], axis=0).astype(np.float32)


def _dilated_attention(q, k, v):
    n_pairs, b, s, L = q.shape
    T = ATTN_TILE
    assert DILATIONS == (1, 4, 16) and 2 * HEAD_DIM == L == V7X_LANES and s % T == 0
    assert all(ATTN_UNROLL % (T // d // ATTN_BLK) == 0 or (d == 1 and (T // ATTN_BLK) % ATTN_UNROLL == 0)
               for d in DILATIONS)
    n_pat = len(DILATIONS)
    tile = pl.BlockSpec((1, 1, T, L), lambda bi, p, t: (p, bi, t, 0))
    return pl.pallas_call(
        _attn_kernel,
        grid=(b, n_pairs, s // T),
        in_specs=[tile, tile, tile, _const_spec((2, 2 * ATTN_BLK, 2 * ATTN_BLK))],
        out_specs=tile,
        out_shape=jax.ShapeDtypeStruct((n_pairs, b, s, L), BF16),
        scratch_shapes=[
            pltpu.VMEM((n_pat, T, L), BF16),
            pltpu.VMEM((n_pat, 2 * T, L), BF16),
            pltpu.VMEM((n_pat, 2 * T, 2 * L), BF16),
            pltpu.VMEM((T, L), F32),
            pltpu.VMEM((n_pat - 1, T, L), F32),
            pltpu.VMEM((n_pat - 1, T, L), F32),
        ],
        compiler_params=_params("parallel", "parallel", "arbitrary"),
        name="attn",
    )(q, k, v, _attn_bias())


def _ret_kernel(q_ref, k_ref, v_ref, g_ref, hm_ref, dm_ref, qd_ref, kd_ref, cd_ref, o_ref, r_ref, *, n_chunks):
    C = RET_CHUNK
    L = V7X_LANES
    n_pairs = N_RET_HEADS // 2

    @pl.when(pl.program_id(1) == 0)
    def _():
        r_ref[...] = jnp.zeros_like(r_ref)

    state = [r_ref[p] for p in range(n_pairs)]
    for c in range(n_chunks):
        rows = slice(c * C, (c + 1) * C)
        for p in range(n_pairs):
            cols = slice(p * L, (p + 1) * L)
            q2 = q_ref[0, rows, cols]
            k2 = k_ref[0, rows, cols]
            v2 = v_ref[0, rows, 2 * p * L:2 * (p + 1) * L]
            q_st = jnp.concatenate([q2 * hm_ref[0], q2 * hm_ref[1]], axis=0)
            s = lax.dot_general(q_st, k2, (((1,), (1,)), ((), ())), preferred_element_type=F32)
            inner = (s * dm_ref[p]).astype(BF16)
            qdec = (q2.astype(F32) * qd_ref[p]).astype(BF16)
            qdec_st = jnp.concatenate([qdec * hm_ref[0], qdec * hm_ref[1]], axis=0)
            lhs = jnp.concatenate([inner, qdec_st], axis=1)
            rhs = jnp.concatenate([v2, state[p].astype(BF16)], axis=0)
            res = jnp.dot(lhs, rhs, preferred_element_type=F32)
            kdec = (k2.astype(F32) * kd_ref[p]).astype(BF16)
            kv = lax.dot_general(kdec, v2, (((0,), (0,)), ((), ())), preferred_element_type=F32)
            state[p] = cd_ref[p] * state[p] + kv
            for hh in range(2):
                h = 2 * p + hh
                o = res[hh * C:(hh + 1) * C, hh * L:(hh + 1) * L]
                rn = o * lax.rsqrt(jnp.mean(o * o, axis=-1, keepdims=True) + EPS)
                gate = g_ref[0, rows, h * RET_V_DIM:(h + 1) * RET_V_DIM]
                o_ref[0, rows, h * RET_V_DIM:(h + 1) * RET_V_DIM] = (_silu(gate) * rn).astype(o_ref.dtype)
    for p in range(n_pairs):
        r_ref[p] = state[p]


def _retention_tables():
    C = RET_CHUNK
    L = V7X_LANES
    H = N_RET_HEADS
    log_g = np.log1p(-(2.0 ** (-5.0 - np.arange(H, dtype=np.float64))))
    idx = np.arange(C, dtype=np.float64)
    diff = idx[:, None] - idx[None, :]
    decay_mask = np.where(diff[None] >= 0, np.exp(log_g[:, None, None] * np.maximum(diff, 0.0)[None]), 0.0)
    k_decay = np.exp(log_g[:, None] * (C - 1 - idx)[None])
    q_decay = np.exp(log_g[:, None] * (idx + 1.0)[None])
    chunk_decay = np.exp(log_g * C)
    lane_pair = lambda a: np.repeat(a.reshape(H // 2, 2, C), RET_QK_DIM, axis=1).transpose(0, 2, 1)
    dm = decay_mask.reshape(H // 2, 2 * C, C)
    cd = np.repeat(chunk_decay.reshape(H // 2, 2), RET_QK_DIM, axis=1)
    cd = np.broadcast_to(cd[:, :, None], (H // 2, L, 2 * L))
    lane = np.arange(L)
    head_mask = np.stack([lane < RET_QK_DIM, lane >= RET_QK_DIM]).astype(np.float32)
    head_mask = np.broadcast_to(head_mask[:, None, :], (2, C, L))
    f32 = lambda a: np.ascontiguousarray(a, dtype=np.float32)
    return jnp.asarray(f32(head_mask), BF16), f32(dm), f32(lane_pair(q_decay)), f32(lane_pair(k_decay)), f32(cd)


def _retention(rq, rk, rv, rg):
    b, s, _ = rq.shape
    tr = RET_TILE
    L = V7X_LANES
    assert s % tr == 0 and tr % RET_CHUNK == 0 and RET_V_DIM == L and 2 * RET_QK_DIM == L and RET_CHUNK == L
    tabs = _retention_tables()
    tile = lambda w: pl.BlockSpec((1, tr, w), lambda bi, i: (bi, i, 0))
    return pl.pallas_call(
        functools.partial(_ret_kernel, n_chunks=tr // RET_CHUNK),
        grid=(b, s // tr),
        in_specs=[tile(RET_QK_WIDTH), tile(RET_QK_WIDTH), tile(RET_WIDTH), tile(RET_WIDTH)]
        + [_const_spec(t.shape) for t in tabs],
        out_specs=tile(RET_WIDTH),
        out_shape=jax.ShapeDtypeStruct((b, s, RET_WIDTH), BF16),
        scratch_shapes=[pltpu.VMEM((N_RET_HEADS // 2, L, 2 * L), F32)],
        compiler_params=_params("parallel", "arbitrary"),
        name="retention",
    )(rq, rk, rv, rg, *tabs)


def _memkv_kernel(m_ref, nw_ref, w_ref, k_ref, v_ref):
    d = m_ref.shape[-1]
    h = _rmsnorm(m_ref[...], nw_ref[...]).astype(BF16)
    k_ref[...] = jnp.dot(h, w_ref[:, 0:d], preferred_element_type=F32).astype(BF16)
    v_ref[...] = jnp.dot(h, w_ref[:, d:], preferred_element_type=F32).astype(BF16)


def _memkv(mem, norm_w, w_ckv):
    t, d = mem.shape
    tm = min(TOKEN_TILE, t)
    assert t % tm == 0
    row = pl.BlockSpec((tm, d), lambda i: (i, 0))
    return pl.pallas_call(
        _memkv_kernel,
        grid=(t // tm,),
        in_specs=[row, _const_spec((1, d)), _const_spec(w_ckv.shape)],
        out_specs=[row, row],
        out_shape=[jax.ShapeDtypeStruct((t, d), BF16)] * 2,
        compiler_params=_params("parallel"),
        name="memkv",
    )(mem, norm_w.reshape(1, d), w_ckv)


def _cross_kernel(x_ref, oa_ref, or_ref, wmix_ref, nw_ref, wq_ref, k_ref, v_ref, wo_ref, o_ref, a_ref):
    o_attn = jnp.concatenate([oa_ref[p, 0] for p in range(oa_ref.shape[0])], axis=-1)
    wa = o_attn.shape[-1]
    x = x_ref[0] + jnp.dot(o_attn, wmix_ref[0:wa, :], preferred_element_type=F32)
    x = x + jnp.dot(or_ref[0], wmix_ref[wa:, :], preferred_element_type=F32)
    d = x.shape[-1]
    hd = d // N_MEM_HEADS
    h = _rmsnorm(x, nw_ref[...]).astype(BF16)
    q = (jnp.dot(h, wq_ref[...], preferred_element_type=F32) * (hd ** -0.5)).astype(BF16)
    for hh in range(N_MEM_HEADS):
        cols = slice(hh * hd, (hh + 1) * hd)
        s = lax.dot_general(q[:, cols], k_ref[0, :, cols], (((1,), (1,)), ((), ())), preferred_element_type=F32)
        m = jnp.max(s, axis=-1, keepdims=True)
        p = jnp.exp(s - m)
        l = jnp.sum(p, axis=-1, keepdims=True)
        pv = jnp.dot(p.astype(BF16), v_ref[0, :, cols], preferred_element_type=F32)
        a_ref[:, cols] = (pv * (1.0 / l)).astype(BF16)
    o_ref[0] = x + jnp.dot(a_ref[...], wo_ref[...], preferred_element_type=F32)


def _cross(x, o_attn, o_ret, w_mix, norm_w, w_cq, mem_k, mem_v, w_co):
    b, s, d = x.shape
    m = mem_k.shape[1]
    tm = TOKEN_TILE
    assert s % tm == 0
    tile = lambda w: pl.BlockSpec((1, tm, w), lambda bi, i: (bi, i, 0))
    kv = pl.BlockSpec((1, m, d), lambda bi, i: (bi, 0, 0))
    return pl.pallas_call(
        _cross_kernel,
        grid=(b, s // tm),
        in_specs=[tile(d), pl.BlockSpec((o_attn.shape[0], 1, tm, o_attn.shape[-1]), lambda bi, i: (0, bi, i, 0)),
                  tile(o_ret.shape[-1]), _const_spec(w_mix.shape),
                  _const_spec((1, d)), _const_spec(w_cq.shape), kv, kv, _const_spec(w_co.shape)],
        out_specs=tile(d),
        out_shape=jax.ShapeDtypeStruct((b, s, d), F32),
        scratch_shapes=[pltpu.VMEM((tm, d), BF16)],
        compiler_params=_params("parallel", "parallel"),
        name="cross",
    )(x, o_attn, o_ret, w_mix, norm_w.reshape(1, d), w_cq, mem_k, mem_v, w_co)


def kernel(x, mem, norm_ffn1, w_ffn1_in, w_ffn1_out, norm_mix, w_in, w_out, norm_cross, norm_mem,
           w_cq, w_ckv, w_co, norm_ffn2, w_ffn2_in, w_ffn2_out, norm_final):
    b, s, d = x.shape
    m = mem.shape[1]
    depth = w_in.shape[0]
    assert depth >= 1
    t = b * s
    bf = lambda a: a.astype(BF16)
    xt = x.reshape(t, d)
    for l in range(depth):
        last = l == depth - 1
        xt = _ffn(xt, norm_ffn1[l], bf(w_ffn1_in[l]), bf(w_ffn1_out[l]))
        aq, ak, av, rq, rk, rv, rg = _inproj(xt, norm_mix[l], bf(w_in[l]), s)
        sh = lambda a: a.reshape(b, s, a.shape[-1])
        pairs = lambda a: a.reshape(a.shape[0], b, s, a.shape[-1])
        o_attn = _dilated_attention(pairs(aq), pairs(ak), pairs(av))
        o_ret = _retention(sh(rq), sh(rk), sh(rv), sh(rg))
        mk, mv = _memkv(mem.reshape(b * m, d), norm_mem[l], bf(w_ckv[l]))
        xt = _cross(xt.reshape(b, s, d), o_attn, o_ret, bf(w_out[l]), norm_cross[l], bf(w_cq[l]),
                    mk.reshape(b, m, d), mv.reshape(b, m, d), bf(w_co[l])).reshape(t, d)
        xt = _ffn(xt, norm_ffn2[l], bf(w_ffn2_in[l]), bf(w_ffn2_out[l]), norm_final if last else None)
    return xt.reshape(b, s, d)
```

```python
import functools
import math

import jax
import jax.numpy as jnp
import numpy as np
from jax import lax
from jax.experimental import pallas as pl
from jax.experimental.pallas import tpu as pltpu

F32 = jnp.float32
BF16 = jnp.bfloat16

HEAD_DIM = 64
N_ATTN_HEADS = 8
ROT_DIM = HEAD_DIM // 4
ROPE_THETA = 500000.0
DILATIONS = (1, 4, 16)
ATTN_BLK = 128
N_RET_HEADS = 4
RET_QK_DIM = 64
RET_V_DIM = 128
RET_CHUNK = 128
RET_THETA = 10000.0
N_MEM_HEADS = 4
EPS = 1e-6
NEG_INF = -1e30
LOG2_E = 1.4426950408889634

ATTN_WIDTH = N_ATTN_HEADS * HEAD_DIM
RET_QK_WIDTH = N_RET_HEADS * RET_QK_DIM
RET_WIDTH = N_RET_HEADS * RET_V_DIM

V7X_LANES = 128
V7X_MXU_DIM = 256
V7X_VMEM_BYTES = 64 * 1024 * 1024
VMEM_LIMIT = V7X_VMEM_BYTES - 8 * 1024 * 1024

TOKEN_TILE = 1024
ATTN_TILE = 2 * ATTN_BLK * max(DILATIONS)
ATTN_UNROLL = 32
RET_TILE = 2048


def _params(*sem):
    return pltpu.CompilerParams(dimension_semantics=sem, vmem_limit_bytes=VMEM_LIMIT)


def _const_spec(shape):
    nd = len(shape)
    return pl.BlockSpec(shape, lambda *_: (0,) * nd, pipeline_mode=pl.Buffered(1))


def _aligned(i, m):
    return i if isinstance(i, int) else pl.multiple_of(i, m)


def _rmsnorm(x, w):
    ms = jnp.mean(x * x, axis=-1, keepdims=True)
    return x * lax.rsqrt(ms + EPS) * w


def _silu(x):
    return x * jax.nn.sigmoid(x)


def _ffn_kernel(x_ref, nw_ref, win_ref, wout_ref, *rest, d_ff, chunk, final_norm):
    if final_norm:
        fw_ref, o_ref, a_ref = rest
    else:
        o_ref, a_ref = rest
    x = x_ref[...]
    h = _rmsnorm(x, nw_ref[...]).astype(BF16)
    for c in range(d_ff // chunk):
        g = jnp.dot(h, win_ref[:, c * chunk:(c + 1) * chunk], preferred_element_type=F32)
        u = jnp.dot(h, win_ref[:, d_ff + c * chunk:d_ff + (c + 1) * chunk], preferred_element_type=F32)
        a_ref[:, c * chunk:(c + 1) * chunk] = (_silu(g) * u).astype(BF16)
    y = x + 0.5 * jnp.dot(a_ref[...], wout_ref[...], preferred_element_type=F32)
    if final_norm:
        y = _rmsnorm(y, fw_ref[...])
    o_ref[...] = y


def _ffn(x, norm_w, w_in, w_out, final_w=None):
    t, d = x.shape
    d_ff = w_out.shape[0]
    tm = TOKEN_TILE
    chunk = V7X_MXU_DIM
    assert t % tm == 0 and d_ff % chunk == 0
    final_norm = final_w is not None
    in_specs = [
        pl.BlockSpec((tm, d), lambda i: (i, 0)),
        _const_spec((1, d)),
        _const_spec((d, 2 * d_ff)),
        _const_spec((d_ff, d)),
    ]
    args = [x, norm_w.reshape(1, d), w_in, w_out]
    if final_norm:
        in_specs.append(_const_spec((1, d)))
        args.append(final_w.reshape(1, d))
    return pl.pallas_call(
        functools.partial(_ffn_kernel, d_ff=d_ff, chunk=chunk, final_norm=final_norm),
        grid=(t // tm,),
        in_specs=in_specs,
        out_specs=pl.BlockSpec((tm, d), lambda i: (i, 0)),
        out_shape=jax.ShapeDtypeStruct((t, d), F32),
        scratch_shapes=[pltpu.VMEM((tm, d_ff), BF16)],
        compiler_params=_params("parallel"),
        name="ffn_final" if final_norm else "ffn",
    )(*args)


def _rotate(y, tab_ref, shift):
    n = y.shape[-1]
    return (y * tab_ref[0]
            + pltpu.roll(y, shift, 1) * tab_ref[1]
            + pltpu.roll(y, n - shift, 1) * tab_ref[2])


def _inproj_kernel(x_ref, nw_ref, w_ref, ta_ref, tr_ref,
                   aq_ref, ak_ref, av_ref, rq_ref, rk_ref, rv_ref, rg_ref):
    h = _rmsnorm(x_ref[...], nw_ref[...]).astype(BF16)

    def proj(lo, width):
        return jnp.dot(h, w_ref[:, lo:lo + width], preferred_element_type=F32)

    L = V7X_LANES
    a_half = ROT_DIM // 2
    r_half = RET_QK_DIM // 2
    y = proj(0, ATTN_WIDTH)
    for c in range(ATTN_WIDTH // L):
        aq_ref[c] = (_rotate(y[:, c * L:(c + 1) * L], ta_ref, a_half) * (HEAD_DIM ** -0.5)) * LOG2_E
    y = proj(ATTN_WIDTH, ATTN_WIDTH)
    for c in range(ATTN_WIDTH // L):
        ak_ref[c] = _rotate(y[:, c * L:(c + 1) * L], ta_ref, a_half)
    y = proj(2 * ATTN_WIDTH, ATTN_WIDTH)
    for c in range(ATTN_WIDTH // L):
        av_ref[c] = y[:, c * L:(c + 1) * L]
    base = 3 * ATTN_WIDTH
    y = proj(base, 2 * RET_QK_WIDTH)
    for c in range(RET_QK_WIDTH // L):
        rq_ref[:, c * L:(c + 1) * L] = _rotate(y[:, c * L:(c + 1) * L], tr_ref, r_half).astype(BF16)
    for c in range(RET_QK_WIDTH // L):
        yc = y[:, RET_QK_WIDTH + c * L:RET_QK_WIDTH + (c + 1) * L]
        rk_ref[:, c * L:(c + 1) * L] = (_rotate(yc, tr_ref, r_half) * (RET_QK_DIM ** -0.5)).astype(BF16)
    base += 2 * RET_QK_WIDTH
    rv_ref[...] = proj(base, RET_WIDTH).astype(BF16)
    rg_ref[...] = proj(base + RET_WIDTH, RET_WIDTH)


def _rotary_tables(seq, rot_dim, head_dim, theta):
    half = rot_dim // 2
    inv = np.exp(-math.log(theta) * np.arange(half, dtype=np.float64) / half)
    ang = np.arange(seq, dtype=np.float64)[:, None] * inv[None, :]
    cos, sin = np.cos(ang), np.sin(ang)
    pad = head_dim - rot_dim
    ones = np.ones((seq, pad))
    zeros = np.zeros((seq, pad))
    zh = np.zeros((seq, half))
    c = np.concatenate([cos, cos, ones], axis=-1)
    s_plus = np.concatenate([zh, sin, zeros], axis=-1)
    s_minus = np.concatenate([-sin, zh, zeros], axis=-1)
    reps = V7X_LANES // head_dim
    return np.stack([np.tile(t, (1, reps)) for t in (c, s_plus, s_minus)], axis=0).astype(np.float32)


def _inproj(x, norm_w, w_in, seq):
    t, d = x.shape
    tm = TOKEN_TILE
    assert t % tm == 0 and seq % tm == 0
    n_s = seq // tm
    tab_a = _rotary_tables(seq, ROT_DIM, HEAD_DIM, ROPE_THETA)
    tab_r = _rotary_tables(seq, RET_QK_DIM, RET_QK_DIM, RET_THETA)
    widths = (RET_QK_WIDTH, RET_QK_WIDTH, RET_WIDTH, RET_WIDTH)
    dtypes = (BF16, BF16, BF16, F32)
    n_pairs = ATTN_WIDTH // V7X_LANES
    pair_spec = pl.BlockSpec((n_pairs, tm, V7X_LANES), lambda i: (0, i, 0))
    pair_shape = jax.ShapeDtypeStruct((n_pairs, t, V7X_LANES), F32)
    tab_spec = pl.BlockSpec((3, tm, V7X_LANES), lambda i: (0, i % n_s, 0))
    return pl.pallas_call(
        _inproj_kernel,
        grid=(t // tm,),
        in_specs=[
            pl.BlockSpec((tm, d), lambda i: (i, 0)),
            _const_spec((1, d)),
            _const_spec(w_in.shape),
            tab_spec,
            tab_spec,
        ],
        out_specs=[pair_spec] * 3 + [pl.BlockSpec((tm, w), lambda i: (i, 0)) for w in widths],
        out_shape=[pair_shape] * 3 + [jax.ShapeDtypeStruct((t, w), dt) for w, dt in zip(widths, dtypes)],
        compiler_params=_params("parallel"),
        name="inproj",
    )(x, norm_w.reshape(1, d), w_in, tab_a, tab_r)


def _attn_kernel(q_ref, k_ref, v_ref, bias_ref, o_ref,
                 qs_ref, ks_ref, vs_ref, d4_ref, acc_ref, l_ref):
    blk = ATTN_BLK
    L = V7X_LANES
    T = ATTN_TILE
    n_pat = len(DILATIONS)
    t = pl.program_id(2)
    first_tile = (t == 0).astype(jnp.int32)

    def kv_row(d, r, m):
        return r * (T // d + blk) + blk + m

    @pl.when(t == 0)
    def _():
        for pat, d in enumerate(DILATIONS):
            for r in range(d):
                head = slice(kv_row(d, r, -blk), kv_row(d, r, 0))
                ks_ref[pat, head, :] = jnp.zeros((blk, L), BF16)
                vs_ref[pat, head, 0:L] = jnp.zeros((blk, L), BF16)
        vs_ref[:, :, L:2 * L] = jnp.ones(vs_ref.shape[:2] + (L,), BF16)

    @pl.when(t > 0)
    def _():
        for pat, d in enumerate(DILATIONS):
            for r in range(d):
                head = slice(kv_row(d, r, -blk), kv_row(d, r, 0))
                tail = slice(kv_row(d, r, T // d - blk), kv_row(d, r, T // d))
                ks_ref[pat, head, :] = ks_ref[pat, tail, :]
                vs_ref[pat, head, 0:L] = vs_ref[pat, tail, 0:L]

    def put(kind, pat, d, r, m0, x):
        n = x.shape[0]
        if kind == "q":
            qs_ref[pat, r * (T // d) + m0:r * (T // d) + m0 + n, :] = x.astype(BF16)
        elif kind == "k":
            ks_ref[pat, kv_row(d, r, m0):kv_row(d, r, m0) + n, :] = x.astype(BF16)
        else:
            vs_ref[pat, kv_row(d, r, m0):kv_row(d, r, m0) + n, 0:L] = x.astype(BF16)

    ch = 2 * blk
    n4 = T // 4
    n16 = T // 16
    for kind, ref in (("q", q_ref), ("k", k_ref), ("v", v_ref)):
        for c in range(T // ch):
            put(kind, 0, 1, 0, c * ch, ref[0, 0, c * ch:(c + 1) * ch, :])
        for r in range(4):
            for c in range(n4 // ch):
                x = ref[0, 0, pl.ds(r + 4 * ch * c, ch, stride=4), :]
                d4_ref[r * n4 + c * ch:r * n4 + (c + 1) * ch, :] = x
                put(kind, 1, 4, r, c * ch, x)
        for r16 in range(16):
            x = d4_ref[pl.ds((r16 % 4) * n4 + r16 // 4, n16, stride=4), :]
            put(kind, 2, 16, r16, 0, x)

    low = lax.broadcasted_iota(jnp.int32, (blk, L), 1) < HEAD_DIM
    head_a = jnp.where(low, 1.0, 0.0).astype(BF16)
    head_b = jnp.where(low, 0.0, 1.0).astype(BF16)

    def unit(pat, d, r, qb):
        n_r = T // d
        q2 = qs_ref[pat, pl.ds(_aligned(r * n_r + qb * blk, blk), blk), :]
        qq = jnp.concatenate([q2 * head_a, q2 * head_b], axis=0)
        keys = pl.ds(_aligned(kv_row(d, r, (qb - 1) * blk), blk), 2 * blk)
        k2 = ks_ref[pat, keys, :]
        v3 = vs_ref[pat, keys, :]
        no_prev = first_tile * (int(qb == 0) if isinstance(qb, int) else (qb == 0).astype(jnp.int32))
        s = lax.dot_general(qq, k2, (((1,), (1,)), ((), ())), preferred_element_type=F32) + bias_ref[no_prev]
        m = jnp.max(s, axis=-1, keepdims=True)
        p = jnp.exp2(s - m).astype(BF16)
        res = jnp.dot(p, v3, preferred_element_type=F32)
        acc = jnp.where(low, res[0:blk, 0:L], res[blk:, 0:L])
        den = jnp.where(low, res[0:blk, L:], res[blk:, L:])
        m2 = jnp.where(low, m[0:blk], m[blk:])
        if d > 1:
            pos = pl.ds(qb * blk * d + r, blk, stride=d)
            st = pat - 1
            acc_ref[st, pos, :] = acc * (1.0 / den)
            l_ref[st, pos, :] = m2 + jnp.log2(den)
        else:
            pos = pl.ds(_aligned(qb * blk, blk), blk)
            m_tot = m2
            for st in range(n_pat - 1):
                m_tot = jnp.maximum(m_tot, l_ref[st, pos, :])
            w = jnp.exp2(m2 - m_tot)
            acc = w * acc
            den = w * den
            for st in range(n_pat - 1):
                w = jnp.exp2(l_ref[st, pos, :] - m_tot)
                acc = acc + w * acc_ref[st, pos, :]
                den = den + w
            o_ref[0, 0, pos, :] = (acc * (1.0 / den)).astype(o_ref.dtype)

    def run(pat, d):
        n_qb = T // d // blk

        def body(g, carry):
            for j in range(ATTN_UNROLL):
                if n_qb >= ATTN_UNROLL:
                    unit(pat, d, 0, g * ATTN_UNROLL + j)
                else:
                    unit(pat, d, g * (ATTN_UNROLL // n_qb) + j // n_qb, j % n_qb)
            return carry

        lax.fori_loop(0, d * n_qb // ATTN_UNROLL, body, 0)

    for pat in reversed(range(n_pat)):
        run(pat, DILATIONS[pat])


def _attn_bias():
    blk = ATTN_BLK
    qi = np.arange(blk)[:, None]
    kj = np.arange(2 * blk)[None, :]
    dist = blk + qi - kj
    band = (dist >= 0) & (dist <= blk)
    bias = np.where(band, 0.0, NEG_INF)
    bias0 = np.where(band & (kj >= blk), 0.0, NEG_INF)
    return np.stack([np.tile(bias, (2, 1)), np.tile(bias0, (2, 1))], axis=0).astype(np.float32)


def _dilated_attention(q, k, v):
    n_pairs, b, s, L = q.shape
    T = ATTN_TILE
    assert DILATIONS == (1, 4, 16) and 2 * HEAD_DIM == L == V7X_LANES and s % T == 0
    assert all(ATTN_UNROLL % (T // d // ATTN_BLK) == 0 or (d == 1 and (T // ATTN_BLK) % ATTN_UNROLL == 0)
               for d in DILATIONS)
    n_pat = len(DILATIONS)
    tile = pl.BlockSpec((1, 1, T, L), lambda bi, p, t: (p, bi, t, 0))
    return pl.pallas_call(
        _attn_kernel,
        grid=(b, n_pairs, s // T),
        in_specs=[tile, tile, tile, _const_spec((2, 2 * ATTN_BLK, 2 * ATTN_BLK))],
        out_specs=tile,
        out_shape=jax.ShapeDtypeStruct((n_pairs, b, s, L), BF16),
        scratch_shapes=[
            pltpu.VMEM((n_pat, T, L), BF16),
            pltpu.VMEM((n_pat, 2 * T, L), BF16),
            pltpu.VMEM((n_pat, 2 * T, 2 * L), BF16),
            pltpu.VMEM((T, L), F32),
            pltpu.VMEM((n_pat - 1, T, L), F32),
            pltpu.VMEM((n_pat - 1, T, L), F32),
        ],
        compiler_params=_params("parallel", "parallel", "arbitrary"),
        name="attn",
    )(q, k, v, _attn_bias())


def _ret_kernel(q_ref, k_ref, v_ref, g_ref, hm_ref, dm_ref, qd_ref, kd_ref, cd_ref, o_ref, r_ref, *, n_chunks):
    C = RET_CHUNK
    L = V7X_LANES
    n_pairs = N_RET_HEADS // 2

    @pl.when(pl.program_id(1) == 0)
    def _():
        r_ref[...] = jnp.zeros_like(r_ref)

    state = [r_ref[p] for p in range(n_pairs)]
    for c in range(n_chunks):
        rows = slice(c * C, (c + 1) * C)
        for p in range(n_pairs):
            cols = slice(p * L, (p + 1) * L)
            q2 = q_ref[0, rows, cols]
            k2 = k_ref[0, rows, cols]
            v2 = v_ref[0, rows, 2 * p * L:2 * (p + 1) * L]
            q_st = jnp.concatenate([q2 * hm_ref[0], q2 * hm_ref[1]], axis=0)
            s = lax.dot_general(q_st, k2, (((1,), (1,)), ((), ())), preferred_element_type=F32)
            inner = (s * dm_ref[p]).astype(BF16)
            qdec = (q2.astype(F32) * qd_ref[p]).astype(BF16)
            qdec_st = jnp.concatenate([qdec * hm_ref[0], qdec * hm_ref[1]], axis=0)
            lhs = jnp.concatenate([inner, qdec_st], axis=1)
            rhs = jnp.concatenate([v2, state[p].astype(BF16)], axis=0)
            res = jnp.dot(lhs, rhs, preferred_element_type=F32)
            kdec = (k2.astype(F32) * kd_ref[p]).astype(BF16)
            kv = lax.dot_general(kdec, v2, (((0,), (0,)), ((), ())), preferred_element_type=F32)
            state[p] = cd_ref[p] * state[p] + kv
            for hh in range(2):
                h = 2 * p + hh
                o = res[hh * C:(hh + 1) * C, hh * L:(hh + 1) * L]
                rn = o * lax.rsqrt(jnp.mean(o * o, axis=-1, keepdims=True) + EPS)
                gate = g_ref[0, rows, h * RET_V_DIM:(h + 1) * RET_V_DIM]
                o_ref[0, rows, h * RET_V_DIM:(h + 1) * RET_V_DIM] = (_silu(gate) * rn).astype(o_ref.dtype)
    for p in range(n_pairs):
        r_ref[p] = state[p]


def _retention_tables():
    C = RET_CHUNK
    L = V7X_LANES
    H = N_RET_HEADS
    log_g = np.log1p(-(2.0 ** (-5.0 - np.arange(H, dtype=np.float64))))
    idx = np.arange(C, dtype=np.float64)
    diff = idx[:, None] - idx[None, :]
    decay_mask = np.where(diff[None] >= 0, np.exp(log_g[:, None, None] * np.maximum(diff, 0.0)[None]), 0.0)
    k_decay = np.exp(log_g[:, None] * (C - 1 - idx)[None])
    q_decay = np.exp(log_g[:, None] * (idx + 1.0)[None])
    chunk_decay = np.exp(log_g * C)
    lane_pair = lambda a: np.repeat(a.reshape(H // 2, 2, C), RET_QK_DIM, axis=1).transpose(0, 2, 1)
    dm = decay_mask.reshape(H // 2, 2 * C, C)
    cd = np.repeat(chunk_decay.reshape(H // 2, 2), RET_QK_DIM, axis=1)
    cd = np.broadcast_to(cd[:, :, None], (H // 2, L, 2 * L))
    lane = np.arange(L)
    head_mask = np.stack([lane < RET_QK_DIM, lane >= RET_QK_DIM]).astype(np.float32)
    head_mask = np.broadcast_to(head_mask[:, None, :], (2, C, L))
    f32 = lambda a: np.ascontiguousarray(a, dtype=np.float32)
    return jnp.asarray(f32(head_mask), BF16), f32(dm), f32(lane_pair(q_decay)), f32(lane_pair(k_decay)), f32(cd)


def _retention(rq, rk, rv, rg):
    b, s, _ = rq.shape
    tr = RET_TILE
    L = V7X_LANES
    assert s % tr == 0 and tr % RET_CHUNK == 0 and RET_V_DIM == L and 2 * RET_QK_DIM == L and RET_CHUNK == L
    tabs = _retention_tables()
    tile = lambda w: pl.BlockSpec((1, tr, w), lambda bi, i: (bi, i, 0))
    return pl.pallas_call(
        functools.partial(_ret_kernel, n_chunks=tr // RET_CHUNK),
        grid=(b, s // tr),
        in_specs=[tile(RET_QK_WIDTH), tile(RET_QK_WIDTH), tile(RET_WIDTH), tile(RET_WIDTH)]
        + [_const_spec(t.shape) for t in tabs],
        out_specs=tile(RET_WIDTH),
        out_shape=jax.ShapeDtypeStruct((b, s, RET_WIDTH), BF16),
        scratch_shapes=[pltpu.VMEM((N_RET_HEADS // 2, L, 2 * L), F32)],
        compiler_params=_params("parallel", "arbitrary"),
        name="retention",
    )(rq, rk, rv, rg, *tabs)


def _memkv_kernel(m_ref, nw_ref, w_ref, k_ref, v_ref):
    d = m_ref.shape[-1]
    h = _rmsnorm(m_ref[...], nw_ref[...]).astype(BF16)
    k_ref[...] = jnp.dot(h, w_ref[:, 0:d], preferred_element_type=F32).astype(BF16)
    v_ref[...] = jnp.dot(h, w_ref[:, d:], preferred_element_type=F32).astype(BF16)


def _memkv(mem, norm_w, w_ckv):
    t, d = mem.shape
    tm = min(TOKEN_TILE, t)
    assert t % tm == 0
    row = pl.BlockSpec((tm, d), lambda i: (i, 0))
    return pl.pallas_call(
        _memkv_kernel,
        grid=(t // tm,),
        in_specs=[row, _const_spec((1, d)), _const_spec(w_ckv.shape)],
        out_specs=[row, row],
        out_shape=[jax.ShapeDtypeStruct((t, d), BF16)] * 2,
        compiler_params=_params("parallel"),
        name="memkv",
    )(mem, norm_w.reshape(1, d), w_ckv)


def _cross_kernel(x_ref, oa_ref, or_ref, wmix_ref, nw_ref, wq_ref, k_ref, v_ref, wo_ref, o_ref, a_ref):
    o_attn = jnp.concatenate([oa_ref[p, 0] for p in range(oa_ref.shape[0])], axis=-1)
    wa = o_attn.shape[-1]
    x = x_ref[0] + jnp.dot(o_attn, wmix_ref[0:wa, :], preferred_element_type=F32)
    x = x + jnp.dot(or_ref[0], wmix_ref[wa:, :], preferred_element_type=F32)
    d = x.shape[-1]
    hd = d // N_MEM_HEADS
    h = _rmsnorm(x, nw_ref[...]).astype(BF16)
    q = (jnp.dot(h, wq_ref[...], preferred_element_type=F32) * (hd ** -0.5)).astype(BF16)
    for hh in range(N_MEM_HEADS):
        cols = slice(hh * hd, (hh + 1) * hd)
        s = lax.dot_general(q[:, cols], k_ref[0, :, cols], (((1,), (1,)), ((), ())), preferred_element_type=F32)
        m = jnp.max(s, axis=-1, keepdims=True)
        p = jnp.exp(s - m)
        l = jnp.sum(p, axis=-1, keepdims=True)
        pv = jnp.dot(p.astype(BF16), v_ref[0, :, cols], preferred_element_type=F32)
        a_ref[:, cols] = (pv * (1.0 / l)).astype(BF16)
    o_ref[0] = x + jnp.dot(a_ref[...], wo_ref[...], preferred_element_type=F32)


def _cross(x, o_attn, o_ret, w_mix, norm_w, w_cq, mem_k, mem_v, w_co):
    b, s, d = x.shape
    m = mem_k.shape[1]
    tm = TOKEN_TILE
    assert s % tm == 0
    tile = lambda w: pl.BlockSpec((1, tm, w), lambda bi, i: (bi, i, 0))
    kv = pl.BlockSpec((1, m, d), lambda bi, i: (bi, 0, 0))
    return pl.pallas_call(
        _cross_kernel,
        grid=(b, s // tm),
        in_specs=[tile(d), pl.BlockSpec((o_attn.shape[0], 1, tm, o_attn.shape[-1]), lambda bi, i: (0, bi, i, 0)),
                  tile(o_ret.shape[-1]), _const_spec(w_mix.shape),
                  _const_spec((1, d)), _const_spec(w_cq.shape), kv, kv, _const_spec(w_co.shape)],
        out_specs=tile(d),
        out_shape=jax.ShapeDtypeStruct((b, s, d), F32),
        scratch_shapes=[pltpu.VMEM((tm, d), BF16)],
        compiler_params=_params("parallel", "parallel"),
        name="cross",
    )(x, o_attn, o_ret, w_mix, norm_w.reshape(1, d), w_cq, mem_k, mem_v, w_co)


def kernel(x, mem, norm_ffn1, w_ffn1_in, w_ffn1_out, norm_mix, w_in, w_out, norm_cross, norm_mem,
           w_cq, w_ckv, w_co, norm_ffn2, w_ffn2_in, w_ffn2_out, norm_final):
    b, s, d = x.shape
    m = mem.shape[1]
    depth = w_in.shape[0]
    assert depth >= 1
    t = b * s
    bf = lambda a: a.astype(BF16)
    xt = x.reshape(t, d)
    for l in range(depth):
        last = l == depth - 1
        xt = _ffn(xt, norm_ffn1[l], bf(w_ffn1_in[l]), bf(w_ffn1_out[l]))
        aq, ak, av, rq, rk, rv, rg = _inproj(xt, norm_mix[l], bf(w_in[l]), s)
        sh = lambda a: a.reshape(b, s, a.shape[-1])
        pairs = lambda a: a.reshape(a.shape[0], b, s, a.shape[-1])
        o_attn = _dilated_attention(pairs(aq), pairs(ak), pairs(av))
        o_ret = _retention(sh(rq), sh(rk), sh(rv), sh(rg))
        mk, mv = _memkv(mem.reshape(b * m, d), norm_mem[l], bf(w_ckv[l]))
        xt = _cross(xt.reshape(b, s, d), o_attn, o_ret, bf(w_out[l]), norm_cross[l], bf(w_cq[l]),
                    mk.reshape(b, m, d), mv.reshape(b, m, d), bf(w_co[l])).reshape(t, d)
        xt = _ffn(xt, norm_ffn2[l], bf(w_ffn2_in[l]), bf(w_ffn2_out[l]), norm_final if last else None)
    return xt.reshape(b, s, d)
```

```python
import functools
import math

import jax
import jax.numpy as jnp
import numpy as np
from jax import lax
from jax.experimental import pallas as pl
from jax.experimental.pallas import tpu as pltpu

F32 = jnp.float32
BF16 = jnp.bfloat16

HEAD_DIM = 64
N_ATTN_HEADS = 8
ROT_DIM = HEAD_DIM // 4
ROPE_THETA = 500000.0
DILATIONS = (1, 4, 16)
ATTN_BLK = 128
N_RET_HEADS = 4
RET_QK_DIM = 64
RET_V_DIM = 128
RET_CHUNK = 128
RET_THETA = 10000.0
N_MEM_HEADS = 4
EPS = 1e-6
NEG_INF = -1e30
LOG2_E = 1.4426950408889634

ATTN_WIDTH = N_ATTN_HEADS * HEAD_DIM
RET_QK_WIDTH = N_RET_HEADS * RET_QK_DIM
RET_WIDTH = N_RET_HEADS * RET_V_DIM

V7X_LANES = 128
V7X_MXU_DIM = 256
V7X_VMEM_BYTES = 64 * 1024 * 1024
VMEM_LIMIT = V7X_VMEM_BYTES - 8 * 1024 * 1024

TOKEN_TILE = 1024
ATTN_TILE = 2 * ATTN_BLK * max(DILATIONS)
ATTN_UNROLL = 32
RET_TILE = 2048


def _params(*sem):
    return pltpu.CompilerParams(dimension_semantics=sem, vmem_limit_bytes=VMEM_LIMIT)


def _const_spec(shape):
    nd = len(shape)
    return pl.BlockSpec(shape, lambda *_: (0,) * nd, pipeline_mode=pl.Buffered(1))


def _aligned(i, m):
    return i if isinstance(i, int) else pl.multiple_of(i, m)


def _rmsnorm(x, w):
    ms = jnp.mean(x * x, axis=-1, keepdims=True)
    return x * lax.rsqrt(ms + EPS) * w


def _silu(x):
    return x * jax.nn.sigmoid(x)


def _ffn_kernel(x_ref, nw_ref, win_ref, wout_ref, *rest, d_ff, chunk, final_norm):
    if final_norm:
        fw_ref, o_ref, a_ref = rest
    else:
        o_ref, a_ref = rest
    x = x_ref[...]
    h = _rmsnorm(x, nw_ref[...]).astype(BF16)
    for c in range(d_ff // chunk):
        g = jnp.dot(h, win_ref[:, c * chunk:(c + 1) * chunk], preferred_element_type=F32)
        u = jnp.dot(h, win_ref[:, d_ff + c * chunk:d_ff + (c + 1) * chunk], preferred_element_type=F32)
        a_ref[:, c * chunk:(c + 1) * chunk] = (_silu(g) * u).astype(BF16)
    y = x + 0.5 * jnp.dot(a_ref[...], wout_ref[...], preferred_element_type=F32)
    if final_norm:
        y = _rmsnorm(y, fw_ref[...])
    o_ref[...] = y


def _ffn(x, norm_w, w_in, w_out, final_w=None):
    t, d = x.shape
    d_ff = w_out.shape[0]
    tm = TOKEN_TILE
    chunk = V7X_MXU_DIM
    assert t % tm == 0 and d_ff % chunk == 0
    final_norm = final_w is not None
    in_specs = [
        pl.BlockSpec((tm, d), lambda i: (i, 0)),
        _const_spec((1, d)),
        _const_spec((d, 2 * d_ff)),
        _const_spec((d_ff, d)),
    ]
    args = [x, norm_w.reshape(1, d), w_in, w_out]
    if final_norm:
        in_specs.append(_const_spec((1, d)))
        args.append(final_w.reshape(1, d))
    return pl.pallas_call(
        functools.partial(_ffn_kernel, d_ff=d_ff, chunk=chunk, final_norm=final_norm),
        grid=(t // tm,),
        in_specs=in_specs,
        out_specs=pl.BlockSpec((tm, d), lambda i: (i, 0)),
        out_shape=jax.ShapeDtypeStruct((t, d), F32),
        scratch_shapes=[pltpu.VMEM((tm, d_ff), BF16)],
        compiler_params=_params("parallel"),
        name="ffn_final" if final_norm else "ffn",
    )(*args)


def _rotate(y, tab_ref, shift):
    n = y.shape[-1]
    return (y * tab_ref[0]
            + pltpu.roll(y, shift, 1) * tab_ref[1]
            + pltpu.roll(y, n - shift, 1) * tab_ref[2])


def _inproj_kernel(x_ref, nw_ref, w_ref, ta_ref, tr_ref,
                   aq_ref, ak_ref, av_ref, rq_ref, rk_ref, rv_ref, rg_ref):
    h = _rmsnorm(x_ref[...], nw_ref[...]).astype(BF16)

    def proj(lo, width):
        return jnp.dot(h, w_ref[:, lo:lo + width], preferred_element_type=F32)

    L = V7X_LANES
    a_half = ROT_DIM // 2
    r_half = RET_QK_DIM // 2
    y = proj(0, ATTN_WIDTH)
    for c in range(ATTN_WIDTH // L):
        aq_ref[c] = (_rotate(y[:, c * L:(c + 1) * L], ta_ref, a_half) * (HEAD_DIM ** -0.5)) * LOG2_E
    y = proj(ATTN_WIDTH, ATTN_WIDTH)
    for c in range(ATTN_WIDTH // L):
        ak_ref[c] = _rotate(y[:, c * L:(c + 1) * L], ta_ref, a_half)
    y = proj(2 * ATTN_WIDTH, ATTN_WIDTH)
    for c in range(ATTN_WIDTH // L):
        av_ref[c] = y[:, c * L:(c + 1) * L]
    base = 3 * ATTN_WIDTH
    y = proj(base, 2 * RET_QK_WIDTH)
    for c in range(RET_QK_WIDTH // L):
        rq_ref[:, c * L:(c + 1) * L] = _rotate(y[:, c * L:(c + 1) * L], tr_ref, r_half).astype(BF16)
    for c in range(RET_QK_WIDTH // L):
        yc = y[:, RET_QK_WIDTH + c * L:RET_QK_WIDTH + (c + 1) * L]
        rk_ref[:, c * L:(c + 1) * L] = (_rotate(yc, tr_ref, r_half) * (RET_QK_DIM ** -0.5)).astype(BF16)
    base += 2 * RET_QK_WIDTH
    rv_ref[...] = proj(base, RET_WIDTH).astype(BF16)
    rg_ref[...] = proj(base + RET_WIDTH, RET_WIDTH)


def _rotary_tables(seq, rot_dim, head_dim, theta):
    half = rot_dim // 2
    inv = np.exp(-math.log(theta) * np.arange(half, dtype=np.float64) / half)
    ang = np.arange(seq, dtype=np.float64)[:, None] * inv[None, :]
    cos, sin = np.cos(ang), np.sin(ang)
    pad = head_dim - rot_dim
    ones = np.ones((seq, pad))
    zeros = np.zeros((seq, pad))
    zh = np.zeros((seq, half))
    c = np.concatenate([cos, cos, ones], axis=-1)
    s_plus = np.concatenate([zh, sin, zeros], axis=-1)
    s_minus = np.concatenate([-sin, zh, zeros], axis=-1)
    reps = V7X_LANES // head_dim
    return np.stack([np.tile(t, (1, reps)) for t in (c, s_plus, s_minus)], axis=0).astype(np.float32)


def _inproj(x, norm_w, w_in, seq):
    t, d = x.shape
    tm = TOKEN_TILE
    assert t % tm == 0 and seq % tm == 0
    n_s = seq // tm
    tab_a = _rotary_tables(seq, ROT_DIM, HEAD_DIM, ROPE_THETA)
    tab_r = _rotary_tables(seq, RET_QK_DIM, RET_QK_DIM, RET_THETA)
    widths = (RET_QK_WIDTH, RET_QK_WIDTH, RET_WIDTH, RET_WIDTH)
    dtypes = (BF16, BF16, BF16, F32)
    n_pairs = ATTN_WIDTH // V7X_LANES
    pair_spec = pl.BlockSpec((n_pairs, tm, V7X_LANES), lambda i: (0, i, 0))
    pair_shape = jax.ShapeDtypeStruct((n_pairs, t, V7X_LANES), F32)
    tab_spec = pl.BlockSpec((3, tm, V7X_LANES), lambda i: (0, i % n_s, 0))
    return pl.pallas_call(
        _inproj_kernel,
        grid=(t // tm,),
        in_specs=[
            pl.BlockSpec((tm, d), lambda i: (i, 0)),
            _const_spec((1, d)),
            _const_spec(w_in.shape),
            tab_spec,
            tab_spec,
        ],
        out_specs=[pair_spec] * 3 + [pl.BlockSpec((tm, w), lambda i: (i, 0)) for w in widths],
        out_shape=[pair_shape] * 3 + [jax.ShapeDtypeStruct((t, w), dt) for w, dt in zip(widths, dtypes)],
        compiler_params=_params("parallel"),
        name="inproj",
    )(x, norm_w.reshape(1, d), w_in, tab_a, tab_r)


def _attn_kernel(q_ref, k_ref, v_ref, bias_ref, o_ref,
                 qs_ref, ks_ref, vs_ref, d4_ref, acc_ref, l_ref):
    blk = ATTN_BLK
    L = V7X_LANES
    T = ATTN_TILE
    n_pat = len(DILATIONS)
    t = pl.program_id(2)
    first_tile = (t == 0).astype(jnp.int32)

    def kv_row(d, r, m):
        return r * (T // d + blk) + blk + m

    @pl.when(t == 0)
    def _():
        for pat, d in enumerate(DILATIONS):
            for r in range(d):
                head = slice(kv_row(d, r, -blk), kv_row(d, r, 0))
                ks_ref[pat, head, :] = jnp.zeros((blk, L), BF16)
                vs_ref[pat, head, :] = jnp.zeros((blk, L), BF16)

    @pl.when(t > 0)
    def _():
        for pat, d in enumerate(DILATIONS):
            for r in range(d):
                head = slice(kv_row(d, r, -blk), kv_row(d, r, 0))
                tail = slice(kv_row(d, r, T // d - blk), kv_row(d, r, T // d))
                ks_ref[pat, head, :] = ks_ref[pat, tail, :]
                vs_ref[pat, head, :] = vs_ref[pat, tail, :]

    def put(kind, pat, d, r, m0, x):
        n = x.shape[0]
        if kind == "q":
            qs_ref[pat, r * (T // d) + m0:r * (T // d) + m0 + n, :] = x.astype(BF16)
        elif kind == "k":
            ks_ref[pat, kv_row(d, r, m0):kv_row(d, r, m0) + n, :] = x.astype(BF16)
        else:
            vs_ref[pat, kv_row(d, r, m0):kv_row(d, r, m0) + n, :] = x.astype(BF16)

    ch = 2 * blk
    n4 = T // 4
    n16 = T // 16
    for kind, ref in (("q", q_ref), ("k", k_ref), ("v", v_ref)):
        for c in range(T // ch):
            put(kind, 0, 1, 0, c * ch, ref[0, 0, c * ch:(c + 1) * ch, :])
        for r in range(4):
            for c in range(n4 // ch):
                x = ref[0, 0, pl.ds(r + 4 * ch * c, ch, stride=4), :]
                d4_ref[r * n4 + c * ch:r * n4 + (c + 1) * ch, :] = x
                put(kind, 1, 4, r, c * ch, x)
        for r16 in range(16):
            x = d4_ref[pl.ds((r16 % 4) * n4 + r16 // 4, n16, stride=4), :]
            put(kind, 2, 16, r16, 0, x)

    low = lax.broadcasted_iota(jnp.int32, (blk, L), 1) < HEAD_DIM
    head_a = jnp.where(low, 1.0, 0.0).astype(BF16)
    head_b = jnp.where(low, 0.0, 1.0).astype(BF16)
    ones_cols = jnp.ones((2 * blk, L), BF16)

    def unit(pat, d, r, qb):
        n_r = T // d
        q2 = qs_ref[pat, pl.ds(_aligned(r * n_r + qb * blk, blk), blk), :]
        qq = jnp.concatenate([q2 * head_a, q2 * head_b], axis=0)
        keys = pl.ds(_aligned(kv_row(d, r, (qb - 1) * blk), blk), 2 * blk)
        k2 = ks_ref[pat, keys, :]
        v3 = jnp.concatenate([vs_ref[pat, keys, :], ones_cols], axis=1)
        no_prev = first_tile * (int(qb == 0) if isinstance(qb, int) else (qb == 0).astype(jnp.int32))
        band = bias_ref[no_prev]
        s = lax.dot_general(qq, k2, (((1,), (1,)), ((), ())), preferred_element_type=F32)
        s = s + jnp.concatenate([band, band], axis=0)
        m = jnp.max(s, axis=-1, keepdims=True)
        p = jnp.exp2(s - m).astype(BF16)
        res = jnp.dot(p, v3, preferred_element_type=F32)
        acc = jnp.where(low, res[0:blk, 0:L], res[blk:, 0:L])
        den = jnp.where(low, res[0:blk, L:], res[blk:, L:])
        m2 = jnp.where(low, m[0:blk], m[blk:])
        own = pl.ds(_aligned(r * n_r + qb * blk, blk), blk)
        if pat < n_pat - 1:
            o_in = acc_ref[pat, own, :]
            l_in = l_ref[pat, own, :]
            m_tot = jnp.maximum(m2, l_in)
            w = jnp.exp2(m2 - m_tot)
            w_in = jnp.exp2(l_in - m_tot)
            acc = w * acc + w_in * o_in
            den = w * den + w_in
            m2 = m_tot
        out = acc * (1.0 / den)
        if pat > 0:
            d_fine = DILATIONS[pat - 1]
            step = d // d_fine
            start = step * qb * blk + (r if d_fine == 1 else (r % d_fine) * (T // d_fine) + r // d_fine)
            rows = pl.ds(start, blk, stride=step)
            acc_ref[pat - 1, rows, :] = out
            l_ref[pat - 1, rows, :] = m2 + jnp.log2(den)
        else:
            o_ref[0, 0, own, :] = out.astype(o_ref.dtype)

    def run(pat, d):
        n_qb = T // d // blk

        def body(g, carry):
            for j in range(ATTN_UNROLL):
                if n_qb >= ATTN_UNROLL:
                    unit(pat, d, 0, g * ATTN_UNROLL + j)
                else:
                    unit(pat, d, g * (ATTN_UNROLL // n_qb) + j // n_qb, j % n_qb)
            return carry

        lax.fori_loop(0, d * n_qb // ATTN_UNROLL, body, 0)

    for pat in reversed(range(n_pat)):
        run(pat, DILATIONS[pat])


def _attn_bias():
    blk = ATTN_BLK
    qi = np.arange(blk)[:, None]
    kj = np.arange(2 * blk)[None, :]
    dist = blk + qi - kj
    band = (dist >= 0) & (dist <= blk)
    bias = np.where(band, 0.0, NEG_INF)
    bias0 = np.where(band & (kj >= blk), 0.0, NEG_INF)
    return np.stack([bias, bias0], axis=0).astype(np.float32)


def _dilated_attention(q, k, v):
    n_pairs, b, s, L = q.shape
    T = ATTN_TILE
    assert DILATIONS == (1, 4, 16) and 2 * HEAD_DIM == L == V7X_LANES and s % T == 0
    assert all(ATTN_UNROLL % (T // d // ATTN_BLK) == 0 or (d == 1 and (T // ATTN_BLK) % ATTN_UNROLL == 0)
               for d in DILATIONS)
    n_pat = len(DILATIONS)
    tile = pl.BlockSpec((1, 1, T, L), lambda bi, p, t: (p, bi, t, 0))
    return pl.pallas_call(
        _attn_kernel,
        grid=(b, n_pairs, s // T),
        in_specs=[tile, tile, tile, _const_spec((2, ATTN_BLK, 2 * ATTN_BLK))],
        out_specs=tile,
        out_shape=jax.ShapeDtypeStruct((n_pairs, b, s, L), BF16),
        scratch_shapes=[
            pltpu.VMEM((n_pat, T, L), BF16),
            pltpu.VMEM((n_pat, 2 * T, L), BF16),
            pltpu.VMEM((n_pat, 2 * T, L), BF16),
            pltpu.VMEM((T, L), F32),
            pltpu.VMEM((n_pat - 1, T, L), F32),
            pltpu.VMEM((n_pat - 1, T, L), F32),
        ],
        compiler_params=_params("parallel", "parallel", "arbitrary"),
        name="attn",
    )(q, k, v, _attn_bias())


def _ret_kernel(q_ref, k_ref, v_ref, g_ref, hm_ref, dm_ref, qd_ref, kd_ref, cd_ref, o_ref, r_ref, *, n_chunks):
    C = RET_CHUNK
    L = V7X_LANES
    n_pairs = N_RET_HEADS // 2

    @pl.when(pl.program_id(1) == 0)
    def _():
        r_ref[...] = jnp.zeros_like(r_ref)

    state = [r_ref[p] for p in range(n_pairs)]
    for c in range(n_chunks):
        rows = slice(c * C, (c + 1) * C)
        for p in range(n_pairs):
            cols = slice(p * L, (p + 1) * L)
            q2 = q_ref[0, rows, cols]
            k2 = k_ref[0, rows, cols]
            v2 = v_ref[0, rows, 2 * p * L:2 * (p + 1) * L]
            q_st = jnp.concatenate([q2 * hm_ref[0], q2 * hm_ref[1]], axis=0)
            s = lax.dot_general(q_st, k2, (((1,), (1,)), ((), ())), preferred_element_type=F32)
            inner = (s * dm_ref[p]).astype(BF16)
            qdec = (q2.astype(F32) * qd_ref[p]).astype(BF16)
            qdec_st = jnp.concatenate([qdec * hm_ref[0], qdec * hm_ref[1]], axis=0)
            lhs = jnp.concatenate([inner, qdec_st], axis=1)
            rhs = jnp.concatenate([v2, state[p].astype(BF16)], axis=0)
            res = jnp.dot(lhs, rhs, preferred_element_type=F32)
            kdec = (k2.astype(F32) * kd_ref[p]).astype(BF16)
            kv = lax.dot_general(kdec, v2, (((0,), (0,)), ((), ())), preferred_element_type=F32)
            state[p] = cd_ref[p] * state[p] + kv
            for hh in range(2):
                h = 2 * p + hh
                o = res[hh * C:(hh + 1) * C, hh * L:(hh + 1) * L]
                rn = o * lax.rsqrt(jnp.mean(o * o, axis=-1, keepdims=True) + EPS)
                gate = g_ref[0, rows, h * RET_V_DIM:(h + 1) * RET_V_DIM]
                o_ref[0, rows, h * RET_V_DIM:(h + 1) * RET_V_DIM] = (_silu(gate) * rn).astype(o_ref.dtype)
    for p in range(n_pairs):
        r_ref[p] = state[p]


def _retention_tables():
    C = RET_CHUNK
    L = V7X_LANES
    H = N_RET_HEADS
    log_g = np.log1p(-(2.0 ** (-5.0 - np.arange(H, dtype=np.float64))))
    idx = np.arange(C, dtype=np.float64)
    diff = idx[:, None] - idx[None, :]
    decay_mask = np.where(diff[None] >= 0, np.exp(log_g[:, None, None] * np.maximum(diff, 0.0)[None]), 0.0)
    k_decay = np.exp(log_g[:, None] * (C - 1 - idx)[None])
    q_decay = np.exp(log_g[:, None] * (idx + 1.0)[None])
    chunk_decay = np.exp(log_g * C)
    lane_pair = lambda a: np.repeat(a.reshape(H // 2, 2, C), RET_QK_DIM, axis=1).transpose(0, 2, 1)
    dm = decay_mask.reshape(H // 2, 2 * C, C)
    cd = np.repeat(chunk_decay.reshape(H // 2, 2), RET_QK_DIM, axis=1)
    cd = np.broadcast_to(cd[:, :, None], (H // 2, L, 2 * L))
    lane = np.arange(L)
    head_mask = np.stack([lane < RET_QK_DIM, lane >= RET_QK_DIM]).astype(np.float32)
    head_mask = np.broadcast_to(head_mask[:, None, :], (2, C, L))
    f32 = lambda a: np.ascontiguousarray(a, dtype=np.float32)
    return jnp.asarray(f32(head_mask), BF16), f32(dm), f32(lane_pair(q_decay)), f32(lane_pair(k_decay)), f32(cd)


def _retention(rq, rk, rv, rg):
    b, s, _ = rq.shape
    tr = RET_TILE
    L = V7X_LANES
    assert s % tr == 0 and tr % RET_CHUNK == 0 and RET_V_DIM == L and 2 * RET_QK_DIM == L and RET_CHUNK == L
    tabs = _retention_tables()
    tile = lambda w: pl.BlockSpec((1, tr, w), lambda bi, i: (bi, i, 0))
    return pl.pallas_call(
        functools.partial(_ret_kernel, n_chunks=tr // RET_CHUNK),
        grid=(b, s // tr),
        in_specs=[tile(RET_QK_WIDTH), tile(RET_QK_WIDTH), tile(RET_WIDTH), tile(RET_WIDTH)]
        + [_const_spec(t.shape) for t in tabs],
        out_specs=tile(RET_WIDTH),
        out_shape=jax.ShapeDtypeStruct((b, s, RET_WIDTH), BF16),
        scratch_shapes=[pltpu.VMEM((N_RET_HEADS // 2, L, 2 * L), F32)],
        compiler_params=_params("parallel", "arbitrary"),
        name="retention",
    )(rq, rk, rv, rg, *tabs)


def _memkv_kernel(m_ref, nw_ref, w_ref, k_ref, v_ref):
    d = m_ref.shape[-1]
    h = _rmsnorm(m_ref[...], nw_ref[...]).astype(BF16)
    k_ref[...] = jnp.dot(h, w_ref[:, 0:d], preferred_element_type=F32).astype(BF16)
    v_ref[...] = jnp.dot(h, w_ref[:, d:], preferred_element_type=F32).astype(BF16)


def _memkv(mem, norm_w, w_ckv):
    t, d = mem.shape
    tm = min(TOKEN_TILE, t)
    assert t % tm == 0
    row = pl.BlockSpec((tm, d), lambda i: (i, 0))
    return pl.pallas_call(
        _memkv_kernel,
        grid=(t // tm,),
        in_specs=[row, _const_spec((1, d)), _const_spec(w_ckv.shape)],
        out_specs=[row, row],
        out_shape=[jax.ShapeDtypeStruct((t, d), BF16)] * 2,
        compiler_params=_params("parallel"),
        name="memkv",
    )(mem, norm_w.reshape(1, d), w_ckv)


def _cross_kernel(x_ref, oa_ref, or_ref, wmix_ref, nw_ref, wq_ref, k_ref, v_ref, wo_ref, o_ref, a_ref):
    o_attn = jnp.concatenate([oa_ref[p, 0] for p in range(oa_ref.shape[0])], axis=-1)
    wa = o_attn.shape[-1]
    x = x_ref[0] + jnp.dot(o_attn, wmix_ref[0:wa, :], preferred_element_type=F32)
    x = x + jnp.dot(or_ref[0], wmix_ref[wa:, :], preferred_element_type=F32)
    d = x.shape[-1]
    hd = d // N_MEM_HEADS
    h = _rmsnorm(x, nw_ref[...]).astype(BF16)
    q = (jnp.dot(h, wq_ref[...], preferred_element_type=F32) * (hd ** -0.5)).astype(BF16)
    for hh in range(N_MEM_HEADS):
        cols = slice(hh * hd, (hh + 1) * hd)
        s = lax.dot_general(q[:, cols], k_ref[0, :, cols], (((1,), (1,)), ((), ())), preferred_element_type=F32)
        m = jnp.max(s, axis=-1, keepdims=True)
        p = jnp.exp(s - m)
        l = jnp.sum(p, axis=-1, keepdims=True)
        pv = jnp.dot(p.astype(BF16), v_ref[0, :, cols], preferred_element_type=F32)
        a_ref[:, cols] = (pv * (1.0 / l)).astype(BF16)
    o_ref[0] = x + jnp.dot(a_ref[...], wo_ref[...], preferred_element_type=F32)


def _cross(x, o_attn, o_ret, w_mix, norm_w, w_cq, mem_k, mem_v, w_co):
    b, s, d = x.shape
    m = mem_k.shape[1]
    tm = TOKEN_TILE
    assert s % tm == 0
    tile = lambda w: pl.BlockSpec((1, tm, w), lambda bi, i: (bi, i, 0))
    kv = pl.BlockSpec((1, m, d), lambda bi, i: (bi, 0, 0))
    return pl.pallas_call(
        _cross_kernel,
        grid=(b, s // tm),
        in_specs=[tile(d), pl.BlockSpec((o_attn.shape[0], 1, tm, o_attn.shape[-1]), lambda bi, i: (0, bi, i, 0)),
                  tile(o_ret.shape[-1]), _const_spec(w_mix.shape),
                  _const_spec((1, d)), _const_spec(w_cq.shape), kv, kv, _const_spec(w_co.shape)],
        out_specs=tile(d),
        out_shape=jax.ShapeDtypeStruct((b, s, d), F32),
        scratch_shapes=[pltpu.VMEM((tm, d), BF16)],
        compiler_params=_params("parallel", "parallel"),
        name="cross",
    )(x, o_attn, o_ret, w_mix, norm_w.reshape(1, d), w_cq, mem_k, mem_v, w_co)


def kernel(x, mem, norm_ffn1, w_ffn1_in, w_ffn1_out, norm_mix, w_in, w_out, norm_cross, norm_mem,
           w_cq, w_ckv, w_co, norm_ffn2, w_ffn2_in, w_ffn2_out, norm_final):
    b, s, d = x.shape
    m = mem.shape[1]
    depth = w_in.shape[0]
    assert depth >= 1
    t = b * s
    bf = lambda a: a.astype(BF16)
    xt = x.reshape(t, d)
    for l in range(depth):
        last = l == depth - 1
        xt = _ffn(xt, norm_ffn1[l], bf(w_ffn1_in[l]), bf(w_ffn1_out[l]))
        aq, ak, av, rq, rk, rv, rg = _inproj(xt, norm_mix[l], bf(w_in[l]), s)
        sh = lambda a: a.reshape(b, s, a.shape[-1])
        pairs = lambda a: a.reshape(a.shape[0], b, s, a.shape[-1])
        o_attn = _dilated_attention(pairs(aq), pairs(ak), pairs(av))
        o_ret = _retention(sh(rq), sh(rk), sh(rv), sh(rg))
        mk, mv = _memkv(mem.reshape(b * m, d), norm_mem[l], bf(w_ckv[l]))
        xt = _cross(xt.reshape(b, s, d), o_attn, o_ret, bf(w_out[l]), norm_cross[l], bf(w_cq[l]),
                    mk.reshape(b, m, d), mv.reshape(b, m, d), bf(w_co[l])).reshape(t, d)
        xt = _ffn(xt, norm_ffn2[l], bf(w_ffn2_in[l]), bf(w_ffn2_out[l]), norm_final if last else None)
    return xt.reshape(b, s, d)
```

```python
import functools
import math

import jax
import jax.numpy as jnp
import numpy as np
from jax import lax
from jax.experimental import pallas as pl
from jax.experimental.pallas import tpu as pltpu

F32 = jnp.float32
BF16 = jnp.bfloat16

HEAD_DIM = 64
N_ATTN_HEADS = 8
ROT_DIM = HEAD_DIM // 4
ROPE_THETA = 500000.0
DILATIONS = (1, 4, 16)
ATTN_BLK = 128
N_RET_HEADS = 4
RET_QK_DIM = 64
RET_V_DIM = 128
RET_CHUNK = 128
RET_THETA = 10000.0
N_MEM_HEADS = 4
EPS = 1e-6
NEG_INF = -1e30
LOG2_E = 1.4426950408889634

ATTN_WIDTH = N_ATTN_HEADS * HEAD_DIM
RET_QK_WIDTH = N_RET_HEADS * RET_QK_DIM
RET_WIDTH = N_RET_HEADS * RET_V_DIM

V7X_LANES = 128
V7X_MXU_DIM = 256
V7X_VMEM_BYTES = 64 * 1024 * 1024
VMEM_LIMIT = V7X_VMEM_BYTES - 8 * 1024 * 1024

TOKEN_TILE = 1024
ATTN_TILE = 2 * ATTN_BLK * max(DILATIONS)
ATTN_UNROLL = 32
RET_TILE = 2048


def _params(*sem):
    return pltpu.CompilerParams(dimension_semantics=sem, vmem_limit_bytes=VMEM_LIMIT)


def _const_spec(shape):
    nd = len(shape)
    return pl.BlockSpec(shape, lambda *_: (0,) * nd, pipeline_mode=pl.Buffered(1))


def _aligned(i, m):
    return i if isinstance(i, int) else pl.multiple_of(i, m)


def _rmsnorm(x, w):
    ms = jnp.mean(x * x, axis=-1, keepdims=True)
    return x * lax.rsqrt(ms + EPS) * w


def _silu(x):
    return x * jax.nn.sigmoid(x)


def _ffn_kernel(x_ref, nw_ref, win_ref, wout_ref, *rest, d_ff, chunk, final_norm):
    if final_norm:
        fw_ref, o_ref, a_ref = rest
    else:
        o_ref, a_ref = rest
    x = x_ref[...]
    h = _rmsnorm(x, nw_ref[...]).astype(BF16)
    for c in range(d_ff // chunk):
        g = jnp.dot(h, win_ref[:, c * chunk:(c + 1) * chunk], preferred_element_type=F32)
        u = jnp.dot(h, win_ref[:, d_ff + c * chunk:d_ff + (c + 1) * chunk], preferred_element_type=F32)
        a_ref[:, c * chunk:(c + 1) * chunk] = (_silu(g) * u).astype(BF16)
    y = x + 0.5 * jnp.dot(a_ref[...], wout_ref[...], preferred_element_type=F32)
    if final_norm:
        y = _rmsnorm(y, fw_ref[...])
    o_ref[...] = y


def _ffn(x, norm_w, w_in, w_out, final_w=None):
    t, d = x.shape
    d_ff = w_out.shape[0]
    tm = TOKEN_TILE
    chunk = V7X_MXU_DIM
    assert t % tm == 0 and d_ff % chunk == 0
    final_norm = final_w is not None
    in_specs = [
        pl.BlockSpec((tm, d), lambda i: (i, 0)),
        _const_spec((1, d)),
        _const_spec((d, 2 * d_ff)),
        _const_spec((d_ff, d)),
    ]
    args = [x, norm_w.reshape(1, d), w_in, w_out]
    if final_norm:
        in_specs.append(_const_spec((1, d)))
        args.append(final_w.reshape(1, d))
    return pl.pallas_call(
        functools.partial(_ffn_kernel, d_ff=d_ff, chunk=chunk, final_norm=final_norm),
        grid=(t // tm,),
        in_specs=in_specs,
        out_specs=pl.BlockSpec((tm, d), lambda i: (i, 0)),
        out_shape=jax.ShapeDtypeStruct((t, d), F32),
        scratch_shapes=[pltpu.VMEM((tm, d_ff), BF16)],
        compiler_params=_params("parallel"),
        name="ffn_final" if final_norm else "ffn",
    )(*args)


def _rotate(y, tab_ref, shift):
    n = y.shape[-1]
    return (y * tab_ref[0]
            + pltpu.roll(y, shift, 1) * tab_ref[1]
            + pltpu.roll(y, n - shift, 1) * tab_ref[2])


def _inproj_kernel(x_ref, nw_ref, w_ref, ta_ref, tr_ref,
                   aq_ref, ak_ref, av_ref, rq_ref, rk_ref, rv_ref, rg_ref):
    h = _rmsnorm(x_ref[...], nw_ref[...]).astype(BF16)

    def proj(lo, width):
        return jnp.dot(h, w_ref[:, lo:lo + width], preferred_element_type=F32)

    L = V7X_LANES
    a_half = ROT_DIM // 2
    r_half = RET_QK_DIM // 2
    y = proj(0, ATTN_WIDTH)
    for c in range(ATTN_WIDTH // L):
        aq_ref[c] = (_rotate(y[:, c * L:(c + 1) * L], ta_ref, a_half) * (HEAD_DIM ** -0.5)) * LOG2_E
    y = proj(ATTN_WIDTH, ATTN_WIDTH)
    for c in range(ATTN_WIDTH // L):
        ak_ref[c] = _rotate(y[:, c * L:(c + 1) * L], ta_ref, a_half)
    y = proj(2 * ATTN_WIDTH, ATTN_WIDTH)
    for c in range(ATTN_WIDTH // L):
        av_ref[c] = y[:, c * L:(c + 1) * L]
    base = 3 * ATTN_WIDTH
    y = proj(base, 2 * RET_QK_WIDTH)
    for c in range(RET_QK_WIDTH // L):
        rq_ref[:, c * L:(c + 1) * L] = _rotate(y[:, c * L:(c + 1) * L], tr_ref, r_half).astype(BF16)
    for c in range(RET_QK_WIDTH // L):
        yc = y[:, RET_QK_WIDTH + c * L:RET_QK_WIDTH + (c + 1) * L]
        rk_ref[:, c * L:(c + 1) * L] = (_rotate(yc, tr_ref, r_half) * (RET_QK_DIM ** -0.5)).astype(BF16)
    base += 2 * RET_QK_WIDTH
    rv_ref[...] = proj(base, RET_WIDTH).astype(BF16)
    rg_ref[...] = proj(base + RET_WIDTH, RET_WIDTH)


def _rotary_tables(seq, rot_dim, head_dim, theta):
    half = rot_dim // 2
    inv = np.exp(-math.log(theta) * np.arange(half, dtype=np.float64) / half)
    ang = np.arange(seq, dtype=np.float64)[:, None] * inv[None, :]
    cos, sin = np.cos(ang), np.sin(ang)
    pad = head_dim - rot_dim
    ones = np.ones((seq, pad))
    zeros = np.zeros((seq, pad))
    zh = np.zeros((seq, half))
    c = np.concatenate([cos, cos, ones], axis=-1)
    s_plus = np.concatenate([zh, sin, zeros], axis=-1)
    s_minus = np.concatenate([-sin, zh, zeros], axis=-1)
    reps = V7X_LANES // head_dim
    return np.stack([np.tile(t, (1, reps)) for t in (c, s_plus, s_minus)], axis=0).astype(np.float32)


def _inproj(x, norm_w, w_in, seq):
    t, d = x.shape
    tm = TOKEN_TILE
    assert t % tm == 0 and seq % tm == 0
    n_s = seq // tm
    tab_a = _rotary_tables(seq, ROT_DIM, HEAD_DIM, ROPE_THETA)
    tab_r = _rotary_tables(seq, RET_QK_DIM, RET_QK_DIM, RET_THETA)
    widths = (RET_QK_WIDTH, RET_QK_WIDTH, RET_WIDTH, RET_WIDTH)
    dtypes = (BF16, BF16, BF16, F32)
    n_pairs = ATTN_WIDTH // V7X_LANES
    pair_spec = pl.BlockSpec((n_pairs, tm, V7X_LANES), lambda i: (0, i, 0))
    pair_shape = jax.ShapeDtypeStruct((n_pairs, t, V7X_LANES), F32)
    tab_spec = pl.BlockSpec((3, tm, V7X_LANES), lambda i: (0, i % n_s, 0))
    return pl.pallas_call(
        _inproj_kernel,
        grid=(t // tm,),
        in_specs=[
            pl.BlockSpec((tm, d), lambda i: (i, 0)),
            _const_spec((1, d)),
            _const_spec(w_in.shape),
            tab_spec,
            tab_spec,
        ],
        out_specs=[pair_spec] * 3 + [pl.BlockSpec((tm, w), lambda i: (i, 0)) for w in widths],
        out_shape=[pair_shape] * 3 + [jax.ShapeDtypeStruct((t, w), dt) for w, dt in zip(widths, dtypes)],
        compiler_params=_params("parallel"),
        name="inproj",
    )(x, norm_w.reshape(1, d), w_in, tab_a, tab_r)


def _attn_kernel(q_ref, k_ref, v_ref, bias_ref, o_ref,
                 qs_ref, ks_ref, vs_ref, d4_ref, acc_ref, l_ref):
    blk = ATTN_BLK
    L = V7X_LANES
    T = ATTN_TILE
    n_pat = len(DILATIONS)
    t = pl.program_id(2)
    first_tile = (t == 0).astype(jnp.int32)

    def kv_row(d, r, m):
        return r * (T // d + blk) + blk + m

    @pl.when(t == 0)
    def _():
        for pat, d in enumerate(DILATIONS):
            for r in range(d):
                head = slice(kv_row(d, r, -blk), kv_row(d, r, 0))
                ks_ref[pat, head, :] = jnp.zeros((blk, L), BF16)
                vs_ref[pat, head, :] = jnp.zeros((blk, L), BF16)

    @pl.when(t > 0)
    def _():
        for pat, d in enumerate(DILATIONS):
            for r in range(d):
                head = slice(kv_row(d, r, -blk), kv_row(d, r, 0))
                tail = slice(kv_row(d, r, T // d - blk), kv_row(d, r, T // d))
                ks_ref[pat, head, :] = ks_ref[pat, tail, :]
                vs_ref[pat, head, :] = vs_ref[pat, tail, :]

    def put(kind, pat, d, r, m0, x):
        n = x.shape[0]
        if kind == "q":
            qs_ref[pat, r * (T // d) + m0:r * (T // d) + m0 + n, :] = x.astype(BF16)
        elif kind == "k":
            ks_ref[pat, kv_row(d, r, m0):kv_row(d, r, m0) + n, :] = x.astype(BF16)
        else:
            vs_ref[pat, kv_row(d, r, m0):kv_row(d, r, m0) + n, :] = x.astype(BF16)

    ch = 2 * blk
    n4 = T // 4
    n16 = T // 16
    for kind, ref in (("q", q_ref), ("k", k_ref), ("v", v_ref)):
        for c in range(T // ch):
            put(kind, 0, 1, 0, c * ch, ref[0, 0, c * ch:(c + 1) * ch, :])
        for r in range(4):
            for c in range(n4 // ch):
                x = ref[0, 0, pl.ds(r + 4 * ch * c, ch, stride=4), :]
                d4_ref[r * n4 + c * ch:r * n4 + (c + 1) * ch, :] = x
                put(kind, 1, 4, r, c * ch, x)
        for r16 in range(16):
            x = d4_ref[pl.ds((r16 % 4) * n4 + r16 // 4, n16, stride=4), :]
            put(kind, 2, 16, r16, 0, x)

    low = lax.broadcasted_iota(jnp.int32, (blk, L), 1) < HEAD_DIM
    head_a = jnp.where(low, 1.0, 0.0).astype(BF16)
    head_b = jnp.where(low, 0.0, 1.0).astype(BF16)
    low2 = lax.broadcasted_iota(jnp.int32, (2 * blk, L), 1) < HEAD_DIM
    head_a2 = jnp.where(low2, 1.0, 0.0).astype(BF16)
    head_b2 = jnp.where(low2, 0.0, 1.0).astype(BF16)

    def unit(pat, d, r, qb):
        n_r = T // d
        q2 = qs_ref[pat, pl.ds(_aligned(r * n_r + qb * blk, blk), blk), :]
        qq = jnp.concatenate([q2 * head_a, q2 * head_b], axis=0)
        keys = pl.ds(_aligned(kv_row(d, r, (qb - 1) * blk), blk), 2 * blk)
        k2 = ks_ref[pat, keys, :]
        no_prev = first_tile * (int(qb == 0) if isinstance(qb, int) else (qb == 0).astype(jnp.int32))
        band = bias_ref[no_prev]
        s = lax.dot_general(qq, k2, (((1,), (1,)), ((), ())), preferred_element_type=F32)
        s = s + jnp.concatenate([band, band], axis=0)
        m = jnp.max(s, axis=-1, keepdims=True)
        p = jnp.exp2(s - m).astype(BF16)
        v2 = vs_ref[pat, keys, :]
        w = jnp.concatenate([jnp.concatenate([v2 * head_a2, head_a2], axis=1),
                             jnp.concatenate([v2 * head_b2, head_b2], axis=1)], axis=0)
        res = jnp.dot(jnp.concatenate([p[0:blk], p[blk:]], axis=1), w, preferred_element_type=F32)
        acc = res[:, 0:L]
        den = res[:, L:]
        m2 = jnp.where(low, m[0:blk], m[blk:])
        own = pl.ds(_aligned(r * n_r + qb * blk, blk), blk)
        if pat < n_pat - 1:
            o_in = acc_ref[pat, own, :]
            l_in = l_ref[pat, own, :]
            m_tot = jnp.maximum(m2, l_in)
            w = jnp.exp2(m2 - m_tot)
            w_in = jnp.exp2(l_in - m_tot)
            acc = w * acc + w_in * o_in
            den = w * den + w_in
            m2 = m_tot
        out = acc * (1.0 / den)
        if pat > 0:
            d_fine = DILATIONS[pat - 1]
            step = d // d_fine
            start = step * qb * blk + (r if d_fine == 1 else (r % d_fine) * (T // d_fine) + r // d_fine)
            rows = pl.ds(start, blk, stride=step)
            acc_ref[pat - 1, rows, :] = out
            l_ref[pat - 1, rows, :] = m2 + jnp.log2(den)
        else:
            o_ref[0, 0, own, :] = out.astype(o_ref.dtype)

    def run(pat, d):
        n_qb = T // d // blk

        def body(g, carry):
            for j in range(ATTN_UNROLL):
                if n_qb >= ATTN_UNROLL:
                    unit(pat, d, 0, g * ATTN_UNROLL + j)
                else:
                    unit(pat, d, g * (ATTN_UNROLL // n_qb) + j // n_qb, j % n_qb)
            return carry

        lax.fori_loop(0, d * n_qb // ATTN_UNROLL, body, 0)

    for pat in reversed(range(n_pat)):
        run(pat, DILATIONS[pat])


def _attn_bias():
    blk = ATTN_BLK
    qi = np.arange(blk)[:, None]
    kj = np.arange(2 * blk)[None, :]
    dist = blk + qi - kj
    band = (dist >= 0) & (dist <= blk)
    bias = np.where(band, 0.0, NEG_INF)
    bias0 = np.where(band & (kj >= blk), 0.0, NEG_INF)
    return np.stack([bias, bias0], axis=0).astype(np.float32)


def _dilated_attention(q, k, v):
    n_pairs, b, s, L = q.shape
    T = ATTN_TILE
    assert DILATIONS == (1, 4, 16) and 2 * HEAD_DIM == L == V7X_LANES and s % T == 0
    assert all(ATTN_UNROLL % (T // d // ATTN_BLK) == 0 or (d == 1 and (T // ATTN_BLK) % ATTN_UNROLL == 0)
               for d in DILATIONS)
    n_pat = len(DILATIONS)
    tile = pl.BlockSpec((1, 1, T, L), lambda bi, p, t: (p, bi, t, 0))
    return pl.pallas_call(
        _attn_kernel,
        grid=(b, n_pairs, s // T),
        in_specs=[tile, tile, tile, _const_spec((2, ATTN_BLK, 2 * ATTN_BLK))],
        out_specs=tile,
        out_shape=jax.ShapeDtypeStruct((n_pairs, b, s, L), BF16),
        scratch_shapes=[
            pltpu.VMEM((n_pat, T, L), BF16),
            pltpu.VMEM((n_pat, 2 * T, L), BF16),
            pltpu.VMEM((n_pat, 2 * T, L), BF16),
            pltpu.VMEM((T, L), F32),
            pltpu.VMEM((n_pat - 1, T, L), F32),
            pltpu.VMEM((n_pat - 1, T, L), F32),
        ],
        compiler_params=_params("parallel", "parallel", "arbitrary"),
        name="attn",
    )(q, k, v, _attn_bias())


def _ret_kernel(q_ref, k_ref, v_ref, g_ref, hm_ref, dm_ref, qd_ref, kd_ref, cd_ref, o_ref, r_ref, *, n_chunks):
    C = RET_CHUNK
    L = V7X_LANES
    n_pairs = N_RET_HEADS // 2

    @pl.when(pl.program_id(1) == 0)
    def _():
        r_ref[...] = jnp.zeros_like(r_ref)

    state = [r_ref[p] for p in range(n_pairs)]
    for c in range(n_chunks):
        rows = slice(c * C, (c + 1) * C)
        for p in range(n_pairs):
            cols = slice(p * L, (p + 1) * L)
            q2 = q_ref[0, rows, cols]
            k2 = k_ref[0, rows, cols]
            v2 = v_ref[0, rows, 2 * p * L:2 * (p + 1) * L]
            q_st = jnp.concatenate([q2 * hm_ref[0], q2 * hm_ref[1]], axis=0)
            s = lax.dot_general(q_st, k2, (((1,), (1,)), ((), ())), preferred_element_type=F32)
            inner = (s * dm_ref[p]).astype(BF16)
            qdec = (q2.astype(F32) * qd_ref[p]).astype(BF16)
            qdec_st = jnp.concatenate([qdec * hm_ref[0], qdec * hm_ref[1]], axis=0)
            lhs = jnp.concatenate([inner, qdec_st], axis=1)
            rhs = jnp.concatenate([v2, state[p].astype(BF16)], axis=0)
            res = jnp.dot(lhs, rhs, preferred_element_type=F32)
            kdec = (k2.astype(F32) * kd_ref[p]).astype(BF16)
            kv = lax.dot_general(kdec, v2, (((0,), (0,)), ((), ())), preferred_element_type=F32)
            state[p] = cd_ref[p] * state[p] + kv
            for hh in range(2):
                h = 2 * p + hh
                o = res[hh * C:(hh + 1) * C, hh * L:(hh + 1) * L]
                rn = o * lax.rsqrt(jnp.mean(o * o, axis=-1, keepdims=True) + EPS)
                gate = g_ref[0, rows, h * RET_V_DIM:(h + 1) * RET_V_DIM]
                o_ref[0, rows, h * RET_V_DIM:(h + 1) * RET_V_DIM] = (_silu(gate) * rn).astype(o_ref.dtype)
    for p in range(n_pairs):
        r_ref[p] = state[p]


def _retention_tables():
    C = RET_CHUNK
    L = V7X_LANES
    H = N_RET_HEADS
    log_g = np.log1p(-(2.0 ** (-5.0 - np.arange(H, dtype=np.float64))))
    idx = np.arange(C, dtype=np.float64)
    diff = idx[:, None] - idx[None, :]
    decay_mask = np.where(diff[None] >= 0, np.exp(log_g[:, None, None] * np.maximum(diff, 0.0)[None]), 0.0)
    k_decay = np.exp(log_g[:, None] * (C - 1 - idx)[None])
    q_decay = np.exp(log_g[:, None] * (idx + 1.0)[None])
    chunk_decay = np.exp(log_g * C)
    lane_pair = lambda a: np.repeat(a.reshape(H // 2, 2, C), RET_QK_DIM, axis=1).transpose(0, 2, 1)
    dm = decay_mask.reshape(H // 2, 2 * C, C)
    cd = np.repeat(chunk_decay.reshape(H // 2, 2), RET_QK_DIM, axis=1)
    cd = np.broadcast_to(cd[:, :, None], (H // 2, L, 2 * L))
    lane = np.arange(L)
    head_mask = np.stack([lane < RET_QK_DIM, lane >= RET_QK_DIM]).astype(np.float32)
    head_mask = np.broadcast_to(head_mask[:, None, :], (2, C, L))
    f32 = lambda a: np.ascontiguousarray(a, dtype=np.float32)
    return jnp.asarray(f32(head_mask), BF16), f32(dm), f32(lane_pair(q_decay)), f32(lane_pair(k_decay)), f32(cd)


def _retention(rq, rk, rv, rg):
    b, s, _ = rq.shape
    tr = RET_TILE
    L = V7X_LANES
    assert s % tr == 0 and tr % RET_CHUNK == 0 and RET_V_DIM == L and 2 * RET_QK_DIM == L and RET_CHUNK == L
    tabs = _retention_tables()
    tile = lambda w: pl.BlockSpec((1, tr, w), lambda bi, i: (bi, i, 0))
    return pl.pallas_call(
        functools.partial(_ret_kernel, n_chunks=tr // RET_CHUNK),
        grid=(b, s // tr),
        in_specs=[tile(RET_QK_WIDTH), tile(RET_QK_WIDTH), tile(RET_WIDTH), tile(RET_WIDTH)]
        + [_const_spec(t.shape) for t in tabs],
        out_specs=tile(RET_WIDTH),
        out_shape=jax.ShapeDtypeStruct((b, s, RET_WIDTH), BF16),
        scratch_shapes=[pltpu.VMEM((N_RET_HEADS // 2, L, 2 * L), F32)],
        compiler_params=_params("parallel", "arbitrary"),
        name="retention",
    )(rq, rk, rv, rg, *tabs)


def _memkv_kernel(m_ref, nw_ref, w_ref, k_ref, v_ref):
    d = m_ref.shape[-1]
    h = _rmsnorm(m_ref[...], nw_ref[...]).astype(BF16)
    k_ref[...] = jnp.dot(h, w_ref[:, 0:d], preferred_element_type=F32).astype(BF16)
    v_ref[...] = jnp.dot(h, w_ref[:, d:], preferred_element_type=F32).astype(BF16)


def _memkv(mem, norm_w, w_ckv):
    t, d = mem.shape
    tm = min(TOKEN_TILE, t)
    assert t % tm == 0
    row = pl.BlockSpec((tm, d), lambda i: (i, 0))
    return pl.pallas_call(
        _memkv_kernel,
        grid=(t // tm,),
        in_specs=[row, _const_spec((1, d)), _const_spec(w_ckv.shape)],
        out_specs=[row, row],
        out_shape=[jax.ShapeDtypeStruct((t, d), BF16)] * 2,
        compiler_params=_params("parallel"),
        name="memkv",
    )(mem, norm_w.reshape(1, d), w_ckv)


def _cross_kernel(x_ref, oa_ref, or_ref, wmix_ref, nw_ref, wq_ref, k_ref, v_ref, wo_ref, o_ref, a_ref):
    o_attn = jnp.concatenate([oa_ref[p, 0] for p in range(oa_ref.shape[0])], axis=-1)
    wa = o_attn.shape[-1]
    x = x_ref[0] + jnp.dot(o_attn, wmix_ref[0:wa, :], preferred_element_type=F32)
    x = x + jnp.dot(or_ref[0], wmix_ref[wa:, :], preferred_element_type=F32)
    d = x.shape[-1]
    hd = d // N_MEM_HEADS
    h = _rmsnorm(x, nw_ref[...]).astype(BF16)
    q = (jnp.dot(h, wq_ref[...], preferred_element_type=F32) * (hd ** -0.5)).astype(BF16)
    for hh in range(N_MEM_HEADS):
        cols = slice(hh * hd, (hh + 1) * hd)
        s = lax.dot_general(q[:, cols], k_ref[0, :, cols], (((1,), (1,)), ((), ())), preferred_element_type=F32)
        m = jnp.max(s, axis=-1, keepdims=True)
        p = jnp.exp(s - m)
        l = jnp.sum(p, axis=-1, keepdims=True)
        pv = jnp.dot(p.astype(BF16), v_ref[0, :, cols], preferred_element_type=F32)
        a_ref[:, cols] = (pv * (1.0 / l)).astype(BF16)
    o_ref[0] = x + jnp.dot(a_ref[...], wo_ref[...], preferred_element_type=F32)


def _cross(x, o_attn, o_ret, w_mix, norm_w, w_cq, mem_k, mem_v, w_co):
    b, s, d = x.shape
    m = mem_k.shape[1]
    tm = TOKEN_TILE
    assert s % tm == 0
    tile = lambda w: pl.BlockSpec((1, tm, w), lambda bi, i: (bi, i, 0))
    kv = pl.BlockSpec((1, m, d), lambda bi, i: (bi, 0, 0))
    return pl.pallas_call(
        _cross_kernel,
        grid=(b, s // tm),
        in_specs=[tile(d), pl.BlockSpec((o_attn.shape[0], 1, tm, o_attn.shape[-1]), lambda bi, i: (0, bi, i, 0)),
                  tile(o_ret.shape[-1]), _const_spec(w_mix.shape),
                  _const_spec((1, d)), _const_spec(w_cq.shape), kv, kv, _const_spec(w_co.shape)],
        out_specs=tile(d),
        out_shape=jax.ShapeDtypeStruct((b, s, d), F32),
        scratch_shapes=[pltpu.VMEM((tm, d), BF16)],
        compiler_params=_params("parallel", "parallel"),
        name="cross",
    )(x, o_attn, o_ret, w_mix, norm_w.reshape(1, d), w_cq, mem_k, mem_v, w_co)


def kernel(x, mem, norm_ffn1, w_ffn1_in, w_ffn1_out, norm_mix, w_in, w_out, norm_cross, norm_mem,
           w_cq, w_ckv, w_co, norm_ffn2, w_ffn2_in, w_ffn2_out, norm_final):
    b, s, d = x.shape
    m = mem.shape[1]
    depth = w_in.shape[0]
    assert depth >= 1
    t = b * s
    bf = lambda a: a.astype(BF16)
    xt = x.reshape(t, d)
    for l in range(depth):
        last = l == depth - 1
        xt = _ffn(xt, norm_ffn1[l], bf(w_ffn1_in[l]), bf(w_ffn1_out[l]))
        aq, ak, av, rq, rk, rv, rg = _inproj(xt, norm_mix[l], bf(w_in[l]), s)
        sh = lambda a: a.reshape(b, s, a.shape[-1])
        pairs = lambda a: a.reshape(a.shape[0], b, s, a.shape[-1])
        o_attn = _dilated_attention(pairs(aq), pairs(ak), pairs(av))
        o_ret = _retention(sh(rq), sh(rk), sh(rv), sh(rg))
        mk, mv = _memkv(mem.reshape(b * m, d), norm_mem[l], bf(w_ckv[l]))
        xt = _cross(xt.reshape(b, s, d), o_attn, o_ret, bf(w_out[l]), norm_cross[l], bf(w_cq[l]),
                    mk.reshape(b, m, d), mv.reshape(b, m, d), bf(w_co[l])).reshape(t, d)
        xt = _ffn(xt, norm_ffn2[l], bf(w_ffn2_in[l]), bf(w_ffn2_out[l]), norm_final if last else None)
    return xt.reshape(b, s, d)
```

```python
import functools
import math

import jax
import jax.numpy as jnp
import numpy as np
from jax import lax
from jax.experimental import pallas as pl
from jax.experimental.pallas import tpu as pltpu

F32 = jnp.float32
BF16 = jnp.bfloat16

HEAD_DIM = 64
N_ATTN_HEADS = 8
ROT_DIM = HEAD_DIM // 4
ROPE_THETA = 500000.0
DILATIONS = (1, 4, 16)
ATTN_BLK = 128
N_RET_HEADS = 4
RET_QK_DIM = 64
RET_V_DIM = 128
RET_CHUNK = 128
RET_THETA = 10000.0
N_MEM_HEADS = 4
EPS = 1e-6
NEG_INF = -1e30
LOG2_E = 1.4426950408889634

ATTN_WIDTH = N_ATTN_HEADS * HEAD_DIM
RET_QK_WIDTH = N_RET_HEADS * RET_QK_DIM
RET_WIDTH = N_RET_HEADS * RET_V_DIM

V7X_LANES = 128
V7X_MXU_DIM = 256
V7X_VMEM_BYTES = 64 * 1024 * 1024
VMEM_LIMIT = V7X_VMEM_BYTES - 8 * 1024 * 1024

TOKEN_TILE = 1024
ATTN_TILE = 2 * ATTN_BLK * max(DILATIONS)
ATTN_UNROLL = 32
RET_TILE = 4096


def _params(*sem):
    return pltpu.CompilerParams(dimension_semantics=sem, vmem_limit_bytes=VMEM_LIMIT)


def _const_spec(shape):
    nd = len(shape)
    return pl.BlockSpec(shape, lambda *_: (0,) * nd, pipeline_mode=pl.Buffered(1))


def _aligned(i, m):
    return i if isinstance(i, int) else pl.multiple_of(i, m)


def _rmsnorm(x, w):
    ms = jnp.mean(x * x, axis=-1, keepdims=True)
    return x * lax.rsqrt(ms + EPS) * w


def _silu(x):
    return x * jax.nn.sigmoid(x)


def _ffn_kernel(x_ref, nw_ref, win_ref, wout_ref, *rest, d_ff, chunk, final_norm):
    if final_norm:
        fw_ref, o_ref, a_ref = rest
    else:
        o_ref, a_ref = rest
    x = x_ref[...]
    h = _rmsnorm(x, nw_ref[...]).astype(BF16)
    for c in range(d_ff // chunk):
        g = jnp.dot(h, win_ref[:, c * chunk:(c + 1) * chunk], preferred_element_type=F32)
        u = jnp.dot(h, win_ref[:, d_ff + c * chunk:d_ff + (c + 1) * chunk], preferred_element_type=F32)
        a_ref[:, c * chunk:(c + 1) * chunk] = (_silu(g) * u).astype(BF16)
    if final_norm:
        half = x.shape[0] // 2
        for i in range(2):
            rows = slice(i * half, (i + 1) * half)
            y = x[rows] + 0.5 * jnp.dot(a_ref[rows, :], wout_ref[...], preferred_element_type=F32)
            o_ref[rows, :] = _rmsnorm(y, fw_ref[...])
    else:
        o_ref[...] = x + 0.5 * jnp.dot(a_ref[...], wout_ref[...], preferred_element_type=F32)


def _ffn(x, norm_w, w_in, w_out, final_w=None):
    t, d = x.shape
    d_ff = w_out.shape[0]
    tm = TOKEN_TILE
    chunk = V7X_MXU_DIM
    assert t % tm == 0 and d_ff % chunk == 0
    final_norm = final_w is not None
    in_specs = [
        pl.BlockSpec((tm, d), lambda i: (i, 0)),
        _const_spec((1, d)),
        _const_spec((d, 2 * d_ff)),
        _const_spec((d_ff, d)),
    ]
    args = [x, norm_w.reshape(1, d), w_in, w_out]
    if final_norm:
        in_specs.append(_const_spec((1, d)))
        args.append(final_w.reshape(1, d))
    return pl.pallas_call(
        functools.partial(_ffn_kernel, d_ff=d_ff, chunk=chunk, final_norm=final_norm),
        grid=(t // tm,),
        in_specs=in_specs,
        out_specs=pl.BlockSpec((tm, d), lambda i: (i, 0)),
        out_shape=jax.ShapeDtypeStruct((t, d), F32),
        scratch_shapes=[pltpu.VMEM((tm, d_ff), BF16)],
        compiler_params=_params("parallel"),
        name="ffn_final" if final_norm else "ffn",
    )(*args)


def _rotate(y, tab_ref, shift):
    n = y.shape[-1]
    return (y * tab_ref[0]
            + pltpu.roll(y, shift, 1) * tab_ref[1]
            + pltpu.roll(y, n - shift, 1) * tab_ref[2])


def _inproj_kernel(x_ref, nw_ref, w_ref, ta_ref, tr_ref,
                   aq_ref, ak_ref, av_ref, rq_ref, rk_ref, rv_ref, rg_ref):
    h = _rmsnorm(x_ref[...], nw_ref[...]).astype(BF16)

    def proj(lo, width):
        return jnp.dot(h, w_ref[:, lo:lo + width], preferred_element_type=F32)

    L = V7X_LANES
    a_half = ROT_DIM // 2
    r_half = RET_QK_DIM // 2
    y = proj(0, ATTN_WIDTH)
    for c in range(ATTN_WIDTH // L):
        aq_ref[c] = (_rotate(y[:, c * L:(c + 1) * L], ta_ref, a_half) * (HEAD_DIM ** -0.5)) * LOG2_E
    y = proj(ATTN_WIDTH, ATTN_WIDTH)
    for c in range(ATTN_WIDTH // L):
        ak_ref[c] = _rotate(y[:, c * L:(c + 1) * L], ta_ref, a_half)
    y = proj(2 * ATTN_WIDTH, ATTN_WIDTH)
    for c in range(ATTN_WIDTH // L):
        av_ref[c] = y[:, c * L:(c + 1) * L]
    base = 3 * ATTN_WIDTH
    y = proj(base, 2 * RET_QK_WIDTH)
    for c in range(RET_QK_WIDTH // L):
        rq_ref[:, c * L:(c + 1) * L] = _rotate(y[:, c * L:(c + 1) * L], tr_ref, r_half).astype(BF16)
    for c in range(RET_QK_WIDTH // L):
        yc = y[:, RET_QK_WIDTH + c * L:RET_QK_WIDTH + (c + 1) * L]
        rk_ref[:, c * L:(c + 1) * L] = (_rotate(yc, tr_ref, r_half) * (RET_QK_DIM ** -0.5)).astype(BF16)
    base += 2 * RET_QK_WIDTH
    rv_ref[...] = proj(base, RET_WIDTH).astype(BF16)
    rg_ref[...] = proj(base + RET_WIDTH, RET_WIDTH)


def _rotary_tables(seq, rot_dim, head_dim, theta):
    half = rot_dim // 2
    inv = np.exp(-math.log(theta) * np.arange(half, dtype=np.float64) / half)
    ang = np.arange(seq, dtype=np.float64)[:, None] * inv[None, :]
    cos, sin = np.cos(ang), np.sin(ang)
    pad = head_dim - rot_dim
    ones = np.ones((seq, pad))
    zeros = np.zeros((seq, pad))
    zh = np.zeros((seq, half))
    c = np.concatenate([cos, cos, ones], axis=-1)
    s_plus = np.concatenate([zh, sin, zeros], axis=-1)
    s_minus = np.concatenate([-sin, zh, zeros], axis=-1)
    reps = V7X_LANES // head_dim
    return np.stack([np.tile(t, (1, reps)) for t in (c, s_plus, s_minus)], axis=0).astype(np.float32)


def _inproj(x, norm_w, w_in, seq):
    t, d = x.shape
    tm = TOKEN_TILE
    assert t % tm == 0 and seq % tm == 0
    n_s = seq // tm
    tab_a = _rotary_tables(seq, ROT_DIM, HEAD_DIM, ROPE_THETA)
    tab_r = _rotary_tables(seq, RET_QK_DIM, RET_QK_DIM, RET_THETA)
    widths = (RET_QK_WIDTH, RET_QK_WIDTH, RET_WIDTH, RET_WIDTH)
    dtypes = (BF16, BF16, BF16, F32)
    n_pairs = ATTN_WIDTH // V7X_LANES
    pair_spec = pl.BlockSpec((n_pairs, tm, V7X_LANES), lambda i: (0, i, 0))
    pair_shape = jax.ShapeDtypeStruct((n_pairs, t, V7X_LANES), F32)
    tab_spec = pl.BlockSpec((3, tm, V7X_LANES), lambda i: (0, i % n_s, 0))
    return pl.pallas_call(
        _inproj_kernel,
        grid=(t // tm,),
        in_specs=[
            pl.BlockSpec((tm, d), lambda i: (i, 0)),
            _const_spec((1, d)),
            _const_spec(w_in.shape),
            tab_spec,
            tab_spec,
        ],
        out_specs=[pair_spec] * 3 + [pl.BlockSpec((tm, w), lambda i: (i, 0)) for w in widths],
        out_shape=[pair_shape] * 3 + [jax.ShapeDtypeStruct((t, w), dt) for w, dt in zip(widths, dtypes)],
        compiler_params=_params("parallel"),
        name="inproj",
    )(x, norm_w.reshape(1, d), w_in, tab_a, tab_r)


def _attn_kernel(q_ref, k_ref, v_ref, bias_ref, o_ref,
                 qs_ref, ks_ref, vs_ref, d4_ref, acc_ref, l_ref):
    blk = ATTN_BLK
    L = V7X_LANES
    T = ATTN_TILE
    n_pat = len(DILATIONS)
    t = pl.program_id(2)
    first_tile = (t == 0).astype(jnp.int32)

    def kv_row(d, r, m):
        return r * (T // d + blk) + blk + m

    @pl.when(t == 0)
    def _():
        for pat, d in enumerate(DILATIONS):
            for r in range(d):
                head = slice(kv_row(d, r, -blk), kv_row(d, r, 0))
                ks_ref[pat, head, :] = jnp.zeros((blk, L), BF16)
                vs_ref[pat, head, :] = jnp.zeros((blk, L), BF16)

    @pl.when(t > 0)
    def _():
        for pat, d in enumerate(DILATIONS):
            for r in range(d):
                head = slice(kv_row(d, r, -blk), kv_row(d, r, 0))
                tail = slice(kv_row(d, r, T // d - blk), kv_row(d, r, T // d))
                ks_ref[pat, head, :] = ks_ref[pat, tail, :]
                vs_ref[pat, head, :] = vs_ref[pat, tail, :]

    def put(kind, pat, d, r, m0, x):
        n = x.shape[0]
        if kind == "q":
            qs_ref[pat, r * (T // d) + m0:r * (T // d) + m0 + n, :] = x.astype(BF16)
        elif kind == "k":
            ks_ref[pat, kv_row(d, r, m0):kv_row(d, r, m0) + n, :] = x.astype(BF16)
        else:
            vs_ref[pat, kv_row(d, r, m0):kv_row(d, r, m0) + n, :] = x.astype(BF16)

    ch = 2 * blk
    n4 = T // 4
    n16 = T // 16
    for kind, ref in (("q", q_ref), ("k", k_ref), ("v", v_ref)):
        for c in range(T // ch):
            put(kind, 0, 1, 0, c * ch, ref[0, 0, c * ch:(c + 1) * ch, :])
        for r in range(4):
            for c in range(n4 // ch):
                x = ref[0, 0, pl.ds(r + 4 * ch * c, ch, stride=4), :]
                d4_ref[r * n4 + c * ch:r * n4 + (c + 1) * ch, :] = x
                put(kind, 1, 4, r, c * ch, x)
        for r16 in range(16):
            x = d4_ref[pl.ds((r16 % 4) * n4 + r16 // 4, n16, stride=4), :]
            put(kind, 2, 16, r16, 0, x)

    low = lax.broadcasted_iota(jnp.int32, (blk, L), 1) < HEAD_DIM
    head_a = jnp.where(low, 1.0, 0.0).astype(BF16)
    head_b = jnp.where(low, 0.0, 1.0).astype(BF16)
    ones_cols = jnp.ones((2 * blk, L), BF16)

    def unit(pat, d, r, qb):
        n_r = T // d
        q2 = qs_ref[pat, pl.ds(_aligned(r * n_r + qb * blk, blk), blk), :]
        qq = jnp.concatenate([q2 * head_a, q2 * head_b], axis=0)
        keys = pl.ds(_aligned(kv_row(d, r, (qb - 1) * blk), blk), 2 * blk)
        k2 = ks_ref[pat, keys, :]
        v3 = jnp.concatenate([vs_ref[pat, keys, :], ones_cols], axis=1)
        no_prev = first_tile * (int(qb == 0) if isinstance(qb, int) else (qb == 0).astype(jnp.int32))
        band = bias_ref[no_prev]
        s = lax.dot_general(qq, k2, (((1,), (1,)), ((), ())), preferred_element_type=F32)
        s = s + jnp.concatenate([band, band], axis=0)
        m = jnp.max(s, axis=-1, keepdims=True)
        p = jnp.exp2((s - m).astype(BF16))
        res = jnp.dot(p, v3, preferred_element_type=F32)
        acc = jnp.where(low, res[0:blk, 0:L], res[blk:, 0:L])
        den = jnp.where(low, res[0:blk, L:], res[blk:, L:])
        m2 = jnp.where(low, m[0:blk], m[blk:])
        own = pl.ds(_aligned(r * n_r + qb * blk, blk), blk)
        if pat < n_pat - 1:
            o_in = acc_ref[pat, own, :]
            l_in = l_ref[pat, own, :]
            m_tot = jnp.maximum(m2, l_in)
            w = jnp.exp2(m2 - m_tot)
            w_in = jnp.exp2(l_in - m_tot)
            acc = w * acc + w_in * o_in
            den = w * den + w_in
            m2 = m_tot
        out = acc * (1.0 / den)
        if pat > 0:
            d_fine = DILATIONS[pat - 1]
            step = d // d_fine
            start = step * qb * blk + (r if d_fine == 1 else (r % d_fine) * (T // d_fine) + r // d_fine)
            rows = pl.ds(start, blk, stride=step)
            acc_ref[pat - 1, rows, :] = out
            l_ref[pat - 1, rows, :] = m2 + jnp.log2(den)
        else:
            o_ref[0, 0, own, :] = out.astype(o_ref.dtype)

    def run(pat, d):
        n_qb = T // d // blk

        def body(g, carry):
            for j in range(ATTN_UNROLL):
                if n_qb >= ATTN_UNROLL:
                    unit(pat, d, 0, g * ATTN_UNROLL + j)
                else:
                    unit(pat, d, g * (ATTN_UNROLL // n_qb) + j // n_qb, j % n_qb)
            return carry

        lax.fori_loop(0, d * n_qb // ATTN_UNROLL, body, 0)

    for pat in reversed(range(n_pat)):
        run(pat, DILATIONS[pat])


def _attn_bias():
    blk = ATTN_BLK
    qi = np.arange(blk)[:, None]
    kj = np.arange(2 * blk)[None, :]
    dist = blk + qi - kj
    band = (dist >= 0) & (dist <= blk)
    bias = np.where(band, 0.0, NEG_INF)
    bias0 = np.where(band & (kj >= blk), 0.0, NEG_INF)
    return np.stack([bias, bias0], axis=0).astype(np.float32)


def _dilated_attention(q, k, v):
    n_pairs, b, s, L = q.shape
    T = ATTN_TILE
    assert DILATIONS == (1, 4, 16) and 2 * HEAD_DIM == L == V7X_LANES and s % T == 0
    assert all(ATTN_UNROLL % (T // d // ATTN_BLK) == 0 or (d == 1 and (T // ATTN_BLK) % ATTN_UNROLL == 0)
               for d in DILATIONS)
    n_pat = len(DILATIONS)
    tile = pl.BlockSpec((1, 1, T, L), lambda bi, p, t: (p, bi, t, 0))
    return pl.pallas_call(
        _attn_kernel,
        grid=(b, n_pairs, s // T),
        in_specs=[tile, tile, tile, _const_spec((2, ATTN_BLK, 2 * ATTN_BLK))],
        out_specs=tile,
        out_shape=jax.ShapeDtypeStruct((n_pairs, b, s, L), BF16),
        scratch_shapes=[
            pltpu.VMEM((n_pat, T, L), BF16),
            pltpu.VMEM((n_pat, 2 * T, L), BF16),
            pltpu.VMEM((n_pat, 2 * T, L), BF16),
            pltpu.VMEM((T, L), F32),
            pltpu.VMEM((n_pat - 1, T, L), F32),
            pltpu.VMEM((n_pat - 1, T, L), F32),
        ],
        compiler_params=_params("parallel", "parallel", "arbitrary"),
        name="attn",
    )(q, k, v, _attn_bias())


def _ret_kernel(q_ref, k_ref, v_ref, g_ref, hm_ref, dm_ref, qd_ref, kd_ref, cd_ref, o_ref, r_ref, *, n_chunks):
    C = RET_CHUNK
    L = V7X_LANES
    n_pairs = N_RET_HEADS // 2

    @pl.when(pl.program_id(1) == 0)
    def _():
        r_ref[...] = jnp.zeros_like(r_ref)

    state = [r_ref[p] for p in range(n_pairs)]
    for c in range(n_chunks):
        rows = slice(c * C, (c + 1) * C)
        for p in range(n_pairs):
            cols = slice(p * L, (p + 1) * L)
            q2 = q_ref[0, rows, cols]
            k2 = k_ref[0, rows, cols]
            v2 = v_ref[0, rows, 2 * p * L:2 * (p + 1) * L]
            q_st = jnp.concatenate([q2 * hm_ref[0], q2 * hm_ref[1]], axis=0)
            s = lax.dot_general(q_st, k2, (((1,), (1,)), ((), ())), preferred_element_type=F32)
            inner = (s * dm_ref[p]).astype(BF16)
            qdec = (q2.astype(F32) * qd_ref[p]).astype(BF16)
            qdec_st = jnp.concatenate([qdec * hm_ref[0], qdec * hm_ref[1]], axis=0)
            lhs = jnp.concatenate([inner, qdec_st], axis=1)
            rhs = jnp.concatenate([v2, state[p].astype(BF16)], axis=0)
            res = jnp.dot(lhs, rhs, preferred_element_type=F32)
            kdec = (k2.astype(F32) * kd_ref[p]).astype(BF16)
            kv = lax.dot_general(kdec, v2, (((0,), (0,)), ((), ())), preferred_element_type=F32)
            state[p] = cd_ref[p] * state[p] + kv
            for hh in range(2):
                h = 2 * p + hh
                o = res[hh * C:(hh + 1) * C, hh * L:(hh + 1) * L]
                rn = o * lax.rsqrt(jnp.mean(o * o, axis=-1, keepdims=True) + EPS)
                gate = g_ref[0, rows, h * RET_V_DIM:(h + 1) * RET_V_DIM]
                o_ref[0, rows, h * RET_V_DIM:(h + 1) * RET_V_DIM] = (_silu(gate) * rn).astype(o_ref.dtype)
    for p in range(n_pairs):
        r_ref[p] = state[p]


def _retention_tables():
    C = RET_CHUNK
    L = V7X_LANES
    H = N_RET_HEADS
    log_g = np.log1p(-(2.0 ** (-5.0 - np.arange(H, dtype=np.float64))))
    idx = np.arange(C, dtype=np.float64)
    diff = idx[:, None] - idx[None, :]
    decay_mask = np.where(diff[None] >= 0, np.exp(log_g[:, None, None] * np.maximum(diff, 0.0)[None]), 0.0)
    k_decay = np.exp(log_g[:, None] * (C - 1 - idx)[None])
    q_decay = np.exp(log_g[:, None] * (idx + 1.0)[None])
    chunk_decay = np.exp(log_g * C)
    lane_pair = lambda a: np.repeat(a.reshape(H // 2, 2, C), RET_QK_DIM, axis=1).transpose(0, 2, 1)
    dm = decay_mask.reshape(H // 2, 2 * C, C)
    cd = np.repeat(chunk_decay.reshape(H // 2, 2), RET_QK_DIM, axis=1)
    cd = np.broadcast_to(cd[:, :, None], (H // 2, L, 2 * L))
    lane = np.arange(L)
    head_mask = np.stack([lane < RET_QK_DIM, lane >= RET_QK_DIM]).astype(np.float32)
    head_mask = np.broadcast_to(head_mask[:, None, :], (2, C, L))
    f32 = lambda a: np.ascontiguousarray(a, dtype=np.float32)
    return jnp.asarray(f32(head_mask), BF16), f32(dm), f32(lane_pair(q_decay)), f32(lane_pair(k_decay)), f32(cd)


def _retention(rq, rk, rv, rg):
    b, s, _ = rq.shape
    tr = RET_TILE
    L = V7X_LANES
    assert s % tr == 0 and tr % RET_CHUNK == 0 and RET_V_DIM == L and 2 * RET_QK_DIM == L and RET_CHUNK == L
    tabs = _retention_tables()
    tile = lambda w: pl.BlockSpec((1, tr, w), lambda bi, i: (bi, i, 0))
    return pl.pallas_call(
        functools.partial(_ret_kernel, n_chunks=tr // RET_CHUNK),
        grid=(b, s // tr),
        in_specs=[tile(RET_QK_WIDTH), tile(RET_QK_WIDTH), tile(RET_WIDTH), tile(RET_WIDTH)]
        + [_const_spec(t.shape) for t in tabs],
        out_specs=tile(RET_WIDTH),
        out_shape=jax.ShapeDtypeStruct((b, s, RET_WIDTH), BF16),
        scratch_shapes=[pltpu.VMEM((N_RET_HEADS // 2, L, 2 * L), F32)],
        compiler_params=_params("parallel", "arbitrary"),
        name="retention",
    )(rq, rk, rv, rg, *tabs)


def _memkv_kernel(m_ref, nw_ref, w_ref, k_ref, v_ref):
    d = m_ref.shape[-1]
    h = _rmsnorm(m_ref[...], nw_ref[...]).astype(BF16)
    k_ref[...] = jnp.dot(h, w_ref[:, 0:d], preferred_element_type=F32).astype(BF16)
    v_ref[...] = jnp.dot(h, w_ref[:, d:], preferred_element_type=F32).astype(BF16)


def _memkv(mem, norm_w, w_ckv):
    t, d = mem.shape
    tm = min(TOKEN_TILE, t)
    assert t % tm == 0
    row = pl.BlockSpec((tm, d), lambda i: (i, 0))
    return pl.pallas_call(
        _memkv_kernel,
        grid=(t // tm,),
        in_specs=[row, _const_spec((1, d)), _const_spec(w_ckv.shape)],
        out_specs=[row, row],
        out_shape=[jax.ShapeDtypeStruct((t, d), BF16)] * 2,
        compiler_params=_params("parallel"),
        name="memkv",
    )(mem, norm_w.reshape(1, d), w_ckv)


def _cross_kernel(x_ref, oa_ref, or_ref, wmix_ref, nw_ref, wq_ref, k_ref, v_ref, wo_ref, o_ref, a_ref):
    o_attn = jnp.concatenate([oa_ref[p, 0] for p in range(oa_ref.shape[0])], axis=-1)
    wa = o_attn.shape[-1]
    x = x_ref[0] + jnp.dot(o_attn, wmix_ref[0:wa, :], preferred_element_type=F32)
    x = x + jnp.dot(or_ref[0], wmix_ref[wa:, :], preferred_element_type=F32)
    d = x.shape[-1]
    hd = d // N_MEM_HEADS
    h = _rmsnorm(x, nw_ref[...]).astype(BF16)
    q = (jnp.dot(h, wq_ref[...], preferred_element_type=F32) * (hd ** -0.5)).astype(BF16)
    for hh in range(N_MEM_HEADS):
        cols = slice(hh * hd, (hh + 1) * hd)
        s = lax.dot_general(q[:, cols], k_ref[0, :, cols], (((1,), (1,)), ((), ())), preferred_element_type=F32)
        m = jnp.max(s, axis=-1, keepdims=True)
        p = jnp.exp(s - m)
        l = jnp.sum(p, axis=-1, keepdims=True)
        pv = jnp.dot(p.astype(BF16), v_ref[0, :, cols], preferred_element_type=F32)
        a_ref[:, cols] = (pv * (1.0 / l)).astype(BF16)
    o_ref[0] = x + jnp.dot(a_ref[...], wo_ref[...], preferred_element_type=F32)


def _cross(x, o_attn, o_ret, w_mix, norm_w, w_cq, mem_k, mem_v, w_co):
    b, s, d = x.shape
    m = mem_k.shape[1]
    tm = TOKEN_TILE
    assert s % tm == 0
    tile = lambda w: pl.BlockSpec((1, tm, w), lambda bi, i: (bi, i, 0))
    kv = pl.BlockSpec((1, m, d), lambda bi, i: (bi, 0, 0))
    return pl.pallas_call(
        _cross_kernel,
        grid=(b, s // tm),
        in_specs=[tile(d), pl.BlockSpec((o_attn.shape[0], 1, tm, o_attn.shape[-1]), lambda bi, i: (0, bi, i, 0)),
                  tile(o_ret.shape[-1]), _const_spec(w_mix.shape),
                  _const_spec((1, d)), _const_spec(w_cq.shape), kv, kv, _const_spec(w_co.shape)],
        out_specs=tile(d),
        out_shape=jax.ShapeDtypeStruct((b, s, d), F32),
        scratch_shapes=[pltpu.VMEM((tm, d), BF16)],
        compiler_params=_params("parallel", "parallel"),
        name="cross",
    )(x, o_attn, o_ret, w_mix, norm_w.reshape(1, d), w_cq, mem_k, mem_v, w_co)


def kernel(x, mem, norm_ffn1, w_ffn1_in, w_ffn1_out, norm_mix, w_in, w_out, norm_cross, norm_mem,
           w_cq, w_ckv, w_co, norm_ffn2, w_ffn2_in, w_ffn2_out, norm_final):
    b, s, d = x.shape
    m = mem.shape[1]
    depth = w_in.shape[0]
    assert depth >= 1
    t = b * s
    bf = lambda a: a.astype(BF16)
    xt = x.reshape(t, d)
    for l in range(depth):
        last = l == depth - 1
        xt = _ffn(xt, norm_ffn1[l], bf(w_ffn1_in[l]), bf(w_ffn1_out[l]))
        aq, ak, av, rq, rk, rv, rg = _inproj(xt, norm_mix[l], bf(w_in[l]), s)
        sh = lambda a: a.reshape(b, s, a.shape[-1])
        pairs = lambda a: a.reshape(a.shape[0], b, s, a.shape[-1])
        o_attn = _dilated_attention(pairs(aq), pairs(ak), pairs(av))
        o_ret = _retention(sh(rq), sh(rk), sh(rv), sh(rg))
        mk, mv = _memkv(mem.reshape(b * m, d), norm_mem[l], bf(w_ckv[l]))
        xt = _cross(xt.reshape(b, s, d), o_attn, o_ret, bf(w_out[l]), norm_cross[l], bf(w_cq[l]),
                    mk.reshape(b, m, d), mv.reshape(b, m, d), bf(w_co[l])).reshape(t, d)
        xt = _ffn(xt, norm_ffn2[l], bf(w_ffn2_in[l]), bf(w_ffn2_out[l]), norm_final if last else None)
    return xt.reshape(b, s, d)
```

```python
import functools
import math

import jax
import jax.numpy as jnp
import numpy as np
from jax import lax
from jax.experimental import pallas as pl
from jax.experimental.pallas import tpu as pltpu

F32 = jnp.float32
BF16 = jnp.bfloat16

HEAD_DIM = 64
N_ATTN_HEADS = 8
ROT_DIM = HEAD_DIM // 4
ROPE_THETA = 500000.0
DILATIONS = (1, 4, 16)
ATTN_BLK = 128
N_RET_HEADS = 4
RET_QK_DIM = 64
RET_V_DIM = 128
RET_CHUNK = 128
RET_THETA = 10000.0
N_MEM_HEADS = 4
EPS = 1e-6
NEG_INF = -1e30
LOG2_E = 1.4426950408889634

ATTN_WIDTH = N_ATTN_HEADS * HEAD_DIM
RET_QK_WIDTH = N_RET_HEADS * RET_QK_DIM
RET_WIDTH = N_RET_HEADS * RET_V_DIM

V7X_LANES = 128
V7X_MXU_DIM = 256
V7X_VMEM_BYTES = 64 * 1024 * 1024
VMEM_LIMIT = V7X_VMEM_BYTES - 8 * 1024 * 1024

TOKEN_TILE = 1024
ATTN_TILE = 2 * ATTN_BLK * max(DILATIONS)
ATTN_UNROLL = 32
RET_TILE = 2048


def _params(*sem):
    return pltpu.CompilerParams(dimension_semantics=sem, vmem_limit_bytes=VMEM_LIMIT)


def _const_spec(shape):
    nd = len(shape)
    return pl.BlockSpec(shape, lambda *_: (0,) * nd, pipeline_mode=pl.Buffered(1))


def _aligned(i, m):
    return i if isinstance(i, int) else pl.multiple_of(i, m)


def _rmsnorm(x, w):
    ms = jnp.mean(x * x, axis=-1, keepdims=True)
    return x * lax.rsqrt(ms + EPS) * w


def _silu(x):
    return x * jax.nn.sigmoid(x)


def _ffn_kernel(x_ref, nw_ref, win_ref, wout_ref, *rest, d_ff, chunk, final_norm):
    if final_norm:
        fw_ref, o_ref, a_ref = rest
    else:
        o_ref, a_ref = rest
    x = x_ref[...]
    h = _rmsnorm(x, nw_ref[...]).astype(BF16)
    for c in range(d_ff // chunk):
        g = jnp.dot(h, win_ref[:, c * chunk:(c + 1) * chunk], preferred_element_type=F32)
        u = jnp.dot(h, win_ref[:, d_ff + c * chunk:d_ff + (c + 1) * chunk], preferred_element_type=F32)
        a_ref[:, c * chunk:(c + 1) * chunk] = (_silu(g) * u).astype(BF16)
    y = x + 0.5 * jnp.dot(a_ref[...], wout_ref[...], preferred_element_type=F32)
    if final_norm:
        y = _rmsnorm(y, fw_ref[...])
    o_ref[...] = y


def _ffn(x, norm_w, w_in, w_out, final_w=None):
    t, d = x.shape
    d_ff = w_out.shape[0]
    tm = TOKEN_TILE
    chunk = V7X_MXU_DIM
    assert t % tm == 0 and d_ff % chunk == 0
    final_norm = final_w is not None
    in_specs = [
        pl.BlockSpec((tm, d), lambda i: (i, 0)),
        _const_spec((1, d)),
        _const_spec((d, 2 * d_ff)),
        _const_spec((d_ff, d)),
    ]
    args = [x, norm_w.reshape(1, d), w_in, w_out]
    if final_norm:
        in_specs.append(_const_spec((1, d)))
        args.append(final_w.reshape(1, d))
    return pl.pallas_call(
        functools.partial(_ffn_kernel, d_ff=d_ff, chunk=chunk, final_norm=final_norm),
        grid=(t // tm,),
        in_specs=in_specs,
        out_specs=pl.BlockSpec((tm, d), lambda i: (i, 0)),
        out_shape=jax.ShapeDtypeStruct((t, d), F32),
        scratch_shapes=[pltpu.VMEM((tm, d_ff), BF16)],
        compiler_params=_params("parallel"),
        name="ffn_final" if final_norm else "ffn",
    )(*args)


def _rotate(y, tab_ref, shift):
    n = y.shape[-1]
    return (y * tab_ref[0]
            + pltpu.roll(y, shift, 1) * tab_ref[1]
            + pltpu.roll(y, n - shift, 1) * tab_ref[2])


def _inproj_kernel(x_ref, nw_ref, w_ref, ta_ref, tr_ref,
                   aq_ref, ak_ref, av_ref, rq_ref, rk_ref, rv_ref, rg_ref):
    h = _rmsnorm(x_ref[...], nw_ref[...]).astype(BF16)

    def proj(lo, width):
        return jnp.dot(h, w_ref[:, lo:lo + width], preferred_element_type=F32)

    L = V7X_LANES
    a_half = ROT_DIM // 2
    r_half = RET_QK_DIM // 2
    y = proj(0, ATTN_WIDTH)
    for c in range(ATTN_WIDTH // L):
        aq_ref[c] = (_rotate(y[:, c * L:(c + 1) * L], ta_ref, a_half) * (HEAD_DIM ** -0.5)) * LOG2_E
    y = proj(ATTN_WIDTH, ATTN_WIDTH)
    for c in range(ATTN_WIDTH // L):
        ak_ref[c] = _rotate(y[:, c * L:(c + 1) * L], ta_ref, a_half)
    y = proj(2 * ATTN_WIDTH, ATTN_WIDTH)
    for c in range(ATTN_WIDTH // L):
        av_ref[c] = y[:, c * L:(c + 1) * L]
    base = 3 * ATTN_WIDTH
    y = proj(base, 2 * RET_QK_WIDTH)
    for c in range(RET_QK_WIDTH // L):
        rq_ref[:, c * L:(c + 1) * L] = _rotate(y[:, c * L:(c + 1) * L], tr_ref, r_half).astype(BF16)
    for c in range(RET_QK_WIDTH // L):
        yc = y[:, RET_QK_WIDTH + c * L:RET_QK_WIDTH + (c + 1) * L]
        rk_ref[:, c * L:(c + 1) * L] = (_rotate(yc, tr_ref, r_half) * (RET_QK_DIM ** -0.5)).astype(BF16)
    base += 2 * RET_QK_WIDTH
    rv_ref[...] = proj(base, RET_WIDTH).astype(BF16)
    rg_ref[...] = proj(base + RET_WIDTH, RET_WIDTH)


def _rotary_tables(seq, rot_dim, head_dim, theta):
    half = rot_dim // 2
    inv = np.exp(-math.log(theta) * np.arange(half, dtype=np.float64) / half)
    ang = np.arange(seq, dtype=np.float64)[:, None] * inv[None, :]
    cos, sin = np.cos(ang), np.sin(ang)
    pad = head_dim - rot_dim
    ones = np.ones((seq, pad))
    zeros = np.zeros((seq, pad))
    zh = np.zeros((seq, half))
    c = np.concatenate([cos, cos, ones], axis=-1)
    s_plus = np.concatenate([zh, sin, zeros], axis=-1)
    s_minus = np.concatenate([-sin, zh, zeros], axis=-1)
    reps = V7X_LANES // head_dim
    return np.stack([np.tile(t, (1, reps)) for t in (c, s_plus, s_minus)], axis=0).astype(np.float32)


def _inproj(x, norm_w, w_in, seq):
    t, d = x.shape
    tm = TOKEN_TILE
    assert t % tm == 0 and seq % tm == 0
    n_s = seq // tm
    tab_a = _rotary_tables(seq, ROT_DIM, HEAD_DIM, ROPE_THETA)
    tab_r = _rotary_tables(seq, RET_QK_DIM, RET_QK_DIM, RET_THETA)
    widths = (RET_QK_WIDTH, RET_QK_WIDTH, RET_WIDTH, RET_WIDTH)
    dtypes = (BF16, BF16, BF16, F32)
    n_pairs = ATTN_WIDTH // V7X_LANES
    pair_spec = pl.BlockSpec((n_pairs, tm, V7X_LANES), lambda i: (0, i, 0))
    pair_shape = jax.ShapeDtypeStruct((n_pairs, t, V7X_LANES), F32)
    tab_spec = pl.BlockSpec((3, tm, V7X_LANES), lambda i: (0, i % n_s, 0))
    return pl.pallas_call(
        _inproj_kernel,
        grid=(t // tm,),
        in_specs=[
            pl.BlockSpec((tm, d), lambda i: (i, 0)),
            _const_spec((1, d)),
            _const_spec(w_in.shape),
            tab_spec,
            tab_spec,
        ],
        out_specs=[pair_spec] * 3 + [pl.BlockSpec((tm, w), lambda i: (i, 0)) for w in widths],
        out_shape=[pair_shape] * 3 + [jax.ShapeDtypeStruct((t, w), dt) for w, dt in zip(widths, dtypes)],
        compiler_params=_params("parallel"),
        name="inproj",
    )(x, norm_w.reshape(1, d), w_in, tab_a, tab_r)


def _attn_kernel(q_ref, k_ref, v_ref, bias_ref, o_ref,
                 qs_ref, ks_ref, vs_ref, d4_ref, acc_ref, l_ref):
    blk = ATTN_BLK
    L = V7X_LANES
    T = ATTN_TILE
    n_pat = len(DILATIONS)
    t = pl.program_id(2)
    first_tile = (t == 0).astype(jnp.int32)

    def kv_row(d, r, m):
        return r * (T // d + blk) + blk + m

    @pl.when(t == 0)
    def _():
        for pat, d in enumerate(DILATIONS):
            for r in range(d):
                head = slice(kv_row(d, r, -blk), kv_row(d, r, 0))
                ks_ref[pat, head, :] = jnp.zeros((blk, L), BF16)
                vs_ref[pat, head, :] = jnp.zeros((blk, L), BF16)

    @pl.when(t > 0)
    def _():
        for pat, d in enumerate(DILATIONS):
            for r in range(d):
                head = slice(kv_row(d, r, -blk), kv_row(d, r, 0))
                tail = slice(kv_row(d, r, T // d - blk), kv_row(d, r, T // d))
                ks_ref[pat, head, :] = ks_ref[pat, tail, :]
                vs_ref[pat, head, :] = vs_ref[pat, tail, :]

    def put(kind, pat, d, r, m0, x):
        n = x.shape[0]
        if kind == "q":
            qs_ref[pat, r * (T // d) + m0:r * (T // d) + m0 + n, :] = x.astype(BF16)
        elif kind == "k":
            ks_ref[pat, kv_row(d, r, m0):kv_row(d, r, m0) + n, :] = x.astype(BF16)
        else:
            vs_ref[pat, kv_row(d, r, m0):kv_row(d, r, m0) + n, :] = x.astype(BF16)

    ch = 2 * blk
    n4 = T // 4
    n16 = T // 16
    for kind, ref in (("q", q_ref), ("k", k_ref), ("v", v_ref)):
        for c in range(T // ch):
            put(kind, 0, 1, 0, c * ch, ref[0, 0, c * ch:(c + 1) * ch, :])
        for r in range(4):
            for c in range(n4 // ch):
                x = ref[0, 0, pl.ds(r + 4 * ch * c, ch, stride=4), :]
                d4_ref[r * n4 + c * ch:r * n4 + (c + 1) * ch, :] = x
                put(kind, 1, 4, r, c * ch, x)
        for r16 in range(16):
            x = d4_ref[pl.ds((r16 % 4) * n4 + r16 // 4, n16, stride=4), :]
            put(kind, 2, 16, r16, 0, x)

    low = lax.broadcasted_iota(jnp.int32, (blk, L), 1) < HEAD_DIM
    head_a = jnp.where(low, 1.0, 0.0).astype(BF16)
    head_b = jnp.where(low, 0.0, 1.0).astype(BF16)
    ones_cols = jnp.ones((2 * blk, L), BF16)

    def unit(pat, d, r, qb):
        n_r = T // d
        q2 = qs_ref[pat, pl.ds(_aligned(r * n_r + qb * blk, blk), blk), :]
        qq = jnp.concatenate([q2 * head_a, q2 * head_b], axis=0)
        keys = pl.ds(_aligned(kv_row(d, r, (qb - 1) * blk), blk), 2 * blk)
        k2 = ks_ref[pat, keys, :]
        v3 = jnp.concatenate([vs_ref[pat, keys, :], ones_cols], axis=1)
        no_prev = first_tile * (int(qb == 0) if isinstance(qb, int) else (qb == 0).astype(jnp.int32))
        band = bias_ref[no_prev]
        s = lax.dot_general(qq, k2, (((1,), (1,)), ((), ())), preferred_element_type=F32)
        s = s + jnp.concatenate([band, band], axis=0)
        m = jnp.max(s, axis=-1, keepdims=True)
        p = jnp.exp2(s - m).astype(BF16)
        res = jnp.dot(p, v3, preferred_element_type=F32)
        acc = jnp.where(low, res[0:blk, 0:L], res[blk:, 0:L])
        den = jnp.where(low, res[0:blk, L:], res[blk:, L:])
        m2 = jnp.where(low, m[0:blk], m[blk:])
        own = pl.ds(_aligned(r * n_r + qb * blk, blk), blk)
        if pat < n_pat - 1:
            o_in = acc_ref[pat, own, :]
            l_in = l_ref[pat, own, :]
            m_tot = jnp.maximum(m2, l_in)
            w = jnp.exp2(m2 - m_tot)
            w_in = jnp.exp2(l_in - m_tot)
            acc = w * acc + w_in * o_in
            den = w * den + w_in
            m2 = m_tot
        out = acc * (1.0 / den)
        if pat > 0:
            d_fine = DILATIONS[pat - 1]
            step = d // d_fine
            start = step * qb * blk + (r if d_fine == 1 else (r % d_fine) * (T // d_fine) + r // d_fine)
            rows = pl.ds(start, blk, stride=step)
            acc_ref[pat - 1, rows, :] = out
            l_ref[pat - 1, rows, :] = m2 + jnp.log2(den)
        else:
            o_ref[0, 0, own, :] = out.astype(o_ref.dtype)

    def run(pat, d):
        n_qb = T // d // blk

        def body(g, carry):
            for j in range(ATTN_UNROLL):
                if n_qb >= ATTN_UNROLL:
                    unit(pat, d, 0, g * ATTN_UNROLL + j)
                else:
                    unit(pat, d, g * (ATTN_UNROLL // n_qb) + j // n_qb, j % n_qb)
            return carry

        lax.fori_loop(0, d * n_qb // ATTN_UNROLL, body, 0)

    for pat in reversed(range(n_pat)):
        run(pat, DILATIONS[pat])


def _attn_bias():
    blk = ATTN_BLK
    qi = np.arange(blk)[:, None]
    kj = np.arange(2 * blk)[None, :]
    dist = blk + qi - kj
    band = (dist >= 0) & (dist <= blk)
    bias = np.where(band, 0.0, NEG_INF)
    bias0 = np.where(band & (kj >= blk), 0.0, NEG_INF)
    return np.stack([bias, bias0], axis=0).astype(np.float32)


def _dilated_attention(q, k, v):
    n_pairs, b, s, L = q.shape
    T = ATTN_TILE
    assert DILATIONS == (1, 4, 16) and 2 * HEAD_DIM == L == V7X_LANES and s % T == 0
    assert all(ATTN_UNROLL % (T // d // ATTN_BLK) == 0 or (d == 1 and (T // ATTN_BLK) % ATTN_UNROLL == 0)
               for d in DILATIONS)
    n_pat = len(DILATIONS)
    tile = pl.BlockSpec((1, 1, T, L), lambda bi, p, t: (p, bi, t, 0))
    return pl.pallas_call(
        _attn_kernel,
        grid=(b, n_pairs, s // T),
        in_specs=[tile, tile, tile, _const_spec((2, ATTN_BLK, 2 * ATTN_BLK))],
        out_specs=tile,
        out_shape=jax.ShapeDtypeStruct((n_pairs, b, s, L), BF16),
        scratch_shapes=[
            pltpu.VMEM((n_pat, T, L), BF16),
            pltpu.VMEM((n_pat, 2 * T, L), BF16),
            pltpu.VMEM((n_pat, 2 * T, L), BF16),
            pltpu.VMEM((T, L), F32),
            pltpu.VMEM((n_pat - 1, T, L), F32),
            pltpu.VMEM((n_pat - 1, T, L), F32),
        ],
        compiler_params=_params("parallel", "parallel", "arbitrary"),
        name="attn",
    )(q, k, v, _attn_bias())


def _ret_kernel(q_ref, k_ref, v_ref, g_ref, hm_ref, dm_ref, qd_ref, kd_ref, cd_ref, o_ref, r_ref, *, n_chunks):
    C = RET_CHUNK
    L = V7X_LANES
    n_pairs = N_RET_HEADS // 2

    @pl.when(pl.program_id(1) == 0)
    def _():
        r_ref[...] = jnp.zeros_like(r_ref)

    state = [r_ref[p] for p in range(n_pairs)]
    for c in range(n_chunks):
        rows = slice(c * C, (c + 1) * C)
        for p in range(n_pairs):
            cols = slice(p * L, (p + 1) * L)
            q2 = q_ref[0, rows, cols]
            k2 = k_ref[0, rows, cols]
            v2 = v_ref[0, rows, 2 * p * L:2 * (p + 1) * L]
            q_st = jnp.concatenate([q2 * hm_ref[0], q2 * hm_ref[1]], axis=0)
            s = lax.dot_general(q_st, k2, (((1,), (1,)), ((), ())), preferred_element_type=F32)
            inner = (s * dm_ref[p]).astype(BF16)
            qdec = (q2.astype(F32) * qd_ref[p]).astype(BF16)
            qdec_st = jnp.concatenate([qdec * hm_ref[0], qdec * hm_ref[1]], axis=0)
            lhs = jnp.concatenate([inner, qdec_st], axis=1)
            rhs = jnp.concatenate([v2, state[p].astype(BF16)], axis=0)
            res = jnp.dot(lhs, rhs, preferred_element_type=F32)
            kdec = (k2.astype(F32) * kd_ref[p]).astype(BF16)
            kv = lax.dot_general(kdec, v2, (((0,), (0,)), ((), ())), preferred_element_type=F32)
            state[p] = cd_ref[p] * state[p] + kv
            for hh in range(2):
                h = 2 * p + hh
                o = res[hh * C:(hh + 1) * C, hh * L:(hh + 1) * L]
                rn = o * lax.rsqrt(jnp.mean(o * o, axis=-1, keepdims=True) + EPS)
                gate = g_ref[0, rows, h * RET_V_DIM:(h + 1) * RET_V_DIM]
                o_ref[0, rows, h * RET_V_DIM:(h + 1) * RET_V_DIM] = (_silu(gate) * rn).astype(o_ref.dtype)
    for p in range(n_pairs):
        r_ref[p] = state[p]


def _retention_tables():
    C = RET_CHUNK
    L = V7X_LANES
    H = N_RET_HEADS
    log_g = np.log1p(-(2.0 ** (-5.0 - np.arange(H, dtype=np.float64))))
    idx = np.arange(C, dtype=np.float64)
    diff = idx[:, None] - idx[None, :]
    decay_mask = np.where(diff[None] >= 0, np.exp(log_g[:, None, None] * np.maximum(diff, 0.0)[None]), 0.0)
    k_decay = np.exp(log_g[:, None] * (C - 1 - idx)[None])
    q_decay = np.exp(log_g[:, None] * (idx + 1.0)[None])
    chunk_decay = np.exp(log_g * C)
    lane_pair = lambda a: np.repeat(a.reshape(H // 2, 2, C), RET_QK_DIM, axis=1).transpose(0, 2, 1)
    dm = decay_mask.reshape(H // 2, 2 * C, C)
    cd = np.repeat(chunk_decay.reshape(H // 2, 2), RET_QK_DIM, axis=1)
    cd = np.broadcast_to(cd[:, :, None], (H // 2, L, 2 * L))
    lane = np.arange(L)
    head_mask = np.stack([lane < RET_QK_DIM, lane >= RET_QK_DIM]).astype(np.float32)
    head_mask = np.broadcast_to(head_mask[:, None, :], (2, C, L))
    f32 = lambda a: np.ascontiguousarray(a, dtype=np.float32)
    return jnp.asarray(f32(head_mask), BF16), f32(dm), f32(lane_pair(q_decay)), f32(lane_pair(k_decay)), f32(cd)


def _retention(rq, rk, rv, rg):
    b, s, _ = rq.shape
    tr = RET_TILE
    L = V7X_LANES
    assert s % tr == 0 and tr % RET_CHUNK == 0 and RET_V_DIM == L and 2 * RET_QK_DIM == L and RET_CHUNK == L
    tabs = _retention_tables()
    tile = lambda w: pl.BlockSpec((1, tr, w), lambda bi, i: (bi, i, 0))
    return pl.pallas_call(
        functools.partial(_ret_kernel, n_chunks=tr // RET_CHUNK),
        grid=(b, s // tr),
        in_specs=[tile(RET_QK_WIDTH), tile(RET_QK_WIDTH), tile(RET_WIDTH), tile(RET_WIDTH)]
        + [_const_spec(t.shape) for t in tabs],
        out_specs=tile(RET_WIDTH),
        out_shape=jax.ShapeDtypeStruct((b, s, RET_WIDTH), BF16),
        scratch_shapes=[pltpu.VMEM((N_RET_HEADS // 2, L, 2 * L), F32)],
        compiler_params=_params("parallel", "arbitrary"),
        name="retention",
    )(rq, rk, rv, rg, *tabs)


def _cross_kernel(x_ref, oa_ref, or_ref, wmix_ref, nw_ref, wq_ref, mem_ref, nm_ref, wkv_ref, wo_ref, o_ref,
                  a_ref, k_ref, v_ref):
    d = x_ref.shape[-1]

    @pl.when(pl.program_id(1) == 0)
    def _():
        hm = _rmsnorm(mem_ref[0], nm_ref[...]).astype(BF16)
        k_ref[...] = jnp.dot(hm, wkv_ref[:, 0:d], preferred_element_type=F32).astype(BF16)
        v_ref[...] = jnp.dot(hm, wkv_ref[:, d:], preferred_element_type=F32).astype(BF16)

    o_attn = jnp.concatenate([oa_ref[p, 0] for p in range(oa_ref.shape[0])], axis=-1)
    wa = o_attn.shape[-1]
    x = x_ref[0] + jnp.dot(o_attn, wmix_ref[0:wa, :], preferred_element_type=F32)
    x = x + jnp.dot(or_ref[0], wmix_ref[wa:, :], preferred_element_type=F32)
    hd = d // N_MEM_HEADS
    h = _rmsnorm(x, nw_ref[...]).astype(BF16)
    q = (jnp.dot(h, wq_ref[...], preferred_element_type=F32) * (hd ** -0.5)).astype(BF16)
    for hh in range(N_MEM_HEADS):
        cols = slice(hh * hd, (hh + 1) * hd)
        s = lax.dot_general(q[:, cols], k_ref[:, cols], (((1,), (1,)), ((), ())), preferred_element_type=F32)
        m = jnp.max(s, axis=-1, keepdims=True)
        p = jnp.exp(s - m)
        l = jnp.sum(p, axis=-1, keepdims=True)
        pv = jnp.dot(p.astype(BF16), v_ref[:, cols], preferred_element_type=F32)
        a_ref[:, cols] = (pv * (1.0 / l)).astype(BF16)
    o_ref[0] = x + jnp.dot(a_ref[...], wo_ref[...], preferred_element_type=F32)


def _cross(x, o_attn, o_ret, w_mix, norm_w, w_cq, mem, norm_mem, w_ckv, w_co):
    b, s, d = x.shape
    m = mem.shape[1]
    tm = TOKEN_TILE
    assert s % tm == 0
    tile = lambda w: pl.BlockSpec((1, tm, w), lambda bi, i: (bi, i, 0))
    return pl.pallas_call(
        _cross_kernel,
        grid=(b, s // tm),
        in_specs=[tile(d), pl.BlockSpec((o_attn.shape[0], 1, tm, o_attn.shape[-1]), lambda bi, i: (0, bi, i, 0)),
                  tile(o_ret.shape[-1]), _const_spec(w_mix.shape),
                  _const_spec((1, d)), _const_spec(w_cq.shape),
                  pl.BlockSpec((1, m, d), lambda bi, i: (bi, 0, 0)), _const_spec((1, d)), _const_spec(w_ckv.shape),
                  _const_spec(w_co.shape)],
        out_specs=tile(d),
        out_shape=jax.ShapeDtypeStruct((b, s, d), F32),
        scratch_shapes=[pltpu.VMEM((tm, d), BF16), pltpu.VMEM((m, d), BF16), pltpu.VMEM((m, d), BF16)],
        compiler_params=_params("parallel", "arbitrary"),
        name="cross",
    )(x, o_attn, o_ret, w_mix, norm_w.reshape(1, d), w_cq, mem, norm_mem.reshape(1, d), w_ckv, w_co)


def kernel(x, mem, norm_ffn1, w_ffn1_in, w_ffn1_out, norm_mix, w_in, w_out, norm_cross, norm_mem,
           w_cq, w_ckv, w_co, norm_ffn2, w_ffn2_in, w_ffn2_out, norm_final):
    b, s, d = x.shape
    m = mem.shape[1]
    depth = w_in.shape[0]
    assert depth >= 1
    t = b * s
    bf = lambda a: a.astype(BF16)
    xt = x.reshape(t, d)
    for l in range(depth):
        last = l == depth - 1
        xt = _ffn(xt, norm_ffn1[l], bf(w_ffn1_in[l]), bf(w_ffn1_out[l]))
        aq, ak, av, rq, rk, rv, rg = _inproj(xt, norm_mix[l], bf(w_in[l]), s)
        sh = lambda a: a.reshape(b, s, a.shape[-1])
        pairs = lambda a: a.reshape(a.shape[0], b, s, a.shape[-1])
        o_attn = _dilated_attention(pairs(aq), pairs(ak), pairs(av))
        o_ret = _retention(sh(rq), sh(rk), sh(rv), sh(rg))
        xt = _cross(xt.reshape(b, s, d), o_attn, o_ret, bf(w_out[l]), norm_cross[l], bf(w_cq[l]),
                    mem, norm_mem[l], bf(w_ckv[l]), bf(w_co[l])).reshape(t, d)
        xt = _ffn(xt, norm_ffn2[l], bf(w_ffn2_in[l]), bf(w_ffn2_out[l]), norm_final if last else None)
    return xt.reshape(b, s, d)
```
